```python
import jax, jax.numpy as jnp
from jax import lax
import numpy as np

D_MODEL = 1024
BATCH = 4
SEQ = 8192
DEPTH = 4

CHUNK = 64
N_MIXERS = 2
CONV_WIDTH = 31
RET_HEADS = 4
RET_QK_DIM = D_MODEL // RET_HEADS
RET_V_DIM = 2 * D_MODEL // RET_HEADS
RET_QK_TOTAL = RET_HEADS * RET_QK_DIM
RET_V_TOTAL = RET_HEADS * RET_V_DIM
RET_IN_WIDTH = 2 * RET_QK_TOTAL + 2 * RET_V_TOTAL
D_FF = 4 * D_MODEL
ROPE_BASE = 10000.0
EPS = 1e-6
N_CONV_LAYERS = (DEPTH + 1) // 2
N_RET_LAYERS = DEPTH // 2

kernel_name = "hybrid_conformer_retention_adaln_trunk"


def rmsnorm(x, g):
    xf = x.astype(jnp.float32)
    y = xf * lax.rsqrt(jnp.mean(xf * xf, axis=-1, keepdims=True) + EPS)
    return (y * g.astype(jnp.float32)).astype(x.dtype)


def modulate(h, shift, scale):
    return h * (1.0 + scale[:, None, :]) + shift[:, None, :]


def conformer_conv(h, w_pw1, b_pw1, w_dw, b_dw, ln_g, ln_b, w_pw2, b_pw2):
    u = h @ w_pw1 + b_pw1
    a, g = jnp.split(u, 2, axis=-1)
    u = a * jax.nn.sigmoid(g)
    u = lax.conv_general_dilated(
        u, w_dw[:, None, :], window_strides=(1,), padding=[(CONV_WIDTH - 1, 0)],
        dimension_numbers=('NWC', 'WIO', 'NWC'), feature_group_count=D_MODEL) + b_dw
    uf = u.astype(jnp.float32)
    mu = jnp.mean(uf, axis=-1, keepdims=True)
    var = jnp.mean(jnp.square(uf - mu), axis=-1, keepdims=True)
    u = ((uf - mu) * lax.rsqrt(var + EPS) * ln_g + ln_b).astype(h.dtype)
    u = jax.nn.silu(u)
    return u @ w_pw2 + b_pw2


def rope_tables(seq):
    pos = jnp.arange(seq, dtype=jnp.float32)
    inv = ROPE_BASE ** (-jnp.arange(0, RET_QK_DIM, 2, dtype=jnp.float32) / RET_QK_DIM)
    ang = pos[:, None] * inv[None, :]
    return jnp.cos(ang), jnp.sin(ang)


def apply_rope(x, cos, sin):
    half = RET_QK_DIM // 2
    x1, x2 = x[..., :half], x[..., half:]
    c = cos[None, :, None, :].astype(x.dtype)
    s = sin[None, :, None, :].astype(x.dtype)
    return jnp.concatenate([x1 * c - x2 * s, x2 * c + x1 * s], axis=-1)


def retention(h, w_in, gn_g, gn_b, w_out, cos, sin, log_gamma):
    b, s, _ = h.shape
    nc = s // CHUNK
    proj = h @ w_in
    q, k, v, gate = jnp.split(
        proj, [RET_QK_TOTAL, 2 * RET_QK_TOTAL, 2 * RET_QK_TOTAL + RET_V_TOTAL], axis=-1)
    q = apply_rope(q.reshape(b, s, RET_HEADS, RET_QK_DIM), cos, sin)
    k = apply_rope(k.reshape(b, s, RET_HEADS, RET_QK_DIM), cos, sin) * (RET_QK_DIM ** -0.5)
    v = v.reshape(b, s, RET_HEADS, RET_V_DIM)

    def to_chunks(t):
        return t.reshape(b, nc, CHUNK, RET_HEADS, t.shape[-1]).transpose(0, 1, 3, 2, 4)

    qc, kc, vc = to_chunks(q), to_chunks(k), to_chunks(v)
    idx = jnp.arange(CHUNK, dtype=jnp.float32)
    d_intra = jnp.exp(log_gamma[:, None, None] * jnp.abs(idx[:, None] - idx[None, :]))
    scores = jnp.einsum('bnhcd,bnhed->bnhce', qc, kc) * d_intra.astype(qc.dtype)
    intra = jnp.einsum('bnhce,bnhef->bnhcf', scores, vc)

    xi = jnp.exp(log_gamma[:, None] * (idx + 1.0))
    zeta = jnp.exp(log_gamma[:, None] * (CHUNK - 1.0 - idx))
    chunk_decay = jnp.exp(log_gamma * CHUNK)

    def step(state, inp):
        qj, kj, vj = inp
        cross = jnp.einsum('bhcd,bhdf->bhcf', qj * xi[..., None], state)
        state = state * chunk_decay[:, None, None] + jnp.einsum(
            'bhcd,bhcf->bhdf', kj * zeta[..., None], vj)
        return state, cross

    state0 = jnp.zeros((b, RET_HEADS, RET_QK_DIM, RET_V_DIM), jnp.float32)
    xs = (qc.transpose(1, 0, 2, 3, 4), kc.transpose(1, 0, 2, 3, 4), vc.transpose(1, 0, 2, 3, 4))
    _, cross = lax.scan(step, state0, xs)
    y = intra + cross.transpose(1, 0, 2, 3, 4).astype(intra.dtype)
    y = y.transpose(0, 1, 3, 2, 4).reshape(b, s, RET_HEADS, RET_V_DIM)
    yf = y.astype(jnp.float32)
    mu = jnp.mean(yf, axis=-1, keepdims=True)
    var = jnp.mean(jnp.square(yf - mu), axis=-1, keepdims=True)
    y = ((yf - mu) * lax.rsqrt(var + EPS) * gn_g + gn_b).astype(h.dtype)
    y = jax.nn.silu(gate) * y.reshape(b, s, RET_V_TOTAL)
    return y @ w_out


def setup_inputs(seed: int = 0) -> dict:
    key = jax.random.key(seed)
    ks = jax.random.split(key, 24)
    f32 = jnp.float32
    D = D_MODEL

    def nrm(k, shape, std):
        return jax.random.normal(k, shape, f32) * std

    return {
        "x": nrm(ks[0], (BATCH, SEQ, D), 1.0),
        "c": nrm(ks[1], (BATCH, D), 1.0),
        "ada_w": nrm(ks[2], (DEPTH, D, 6 * D), 0.5 * D ** -0.5),
        "ada_b": nrm(ks[3], (DEPTH, 6 * D), 0.02),
        "norm_mix_g": 1.0 + nrm(ks[4], (DEPTH, D), 0.02),
        "norm_mlp_g": 1.0 + nrm(ks[5], (DEPTH, D), 0.02),
        "conv_w_pw1": nrm(ks[6], (N_CONV_LAYERS, D, 2 * D), D ** -0.5),
        "conv_b_pw1": nrm(ks[7], (N_CONV_LAYERS, 2 * D), 0.02),
        "conv_w_dw": nrm(ks[8], (N_CONV_LAYERS, CONV_WIDTH, D), CONV_WIDTH ** -0.5),
        "conv_b_dw": nrm(ks[9], (N_CONV_LAYERS, D), 0.02),
        "conv_ln_g": 1.0 + nrm(ks[10], (N_CONV_LAYERS, D), 0.02),
        "conv_ln_b": nrm(ks[11], (N_CONV_LAYERS, D), 0.02),
        "conv_w_pw2": nrm(ks[12], (N_CONV_LAYERS, D, D), D ** -0.5),
        "conv_b_pw2": nrm(ks[13], (N_CONV_LAYERS, D), 0.02),
        "ret_w_in": nrm(ks[14], (N_RET_LAYERS, D, RET_IN_WIDTH), D ** -0.5),
        "ret_gn_g": 1.0 + nrm(ks[15], (N_RET_LAYERS, RET_HEADS, RET_V_DIM), 0.02),
        "ret_gn_b": nrm(ks[16], (N_RET_LAYERS, RET_HEADS, RET_V_DIM), 0.02),
        "ret_w_out": nrm(ks[17], (N_RET_LAYERS, RET_V_TOTAL, D), RET_V_TOTAL ** -0.5),
        "mlp_w1": nrm(ks[18], (DEPTH, D, D_FF), D ** -0.5),
        "mlp_w2": nrm(ks[19], (DEPTH, D_FF, D), D_FF ** -0.5),
        "final_norm_g": 1.0 + nrm(ks[20], (D,), 0.02),
    }


def reference(x, c, ada_w, ada_b, norm_mix_g, norm_mlp_g, conv_w_pw1, conv_b_pw1, conv_w_dw,
              conv_b_dw, conv_ln_g, conv_ln_b, conv_w_pw2, conv_b_pw2, ret_w_in, ret_gn_g,
              ret_gn_b, ret_w_out, mlp_w1, mlp_w2, final_norm_g):
    seq = x.shape[1]
    cos, sin = rope_tables(seq)
    log_gamma = jnp.log(1.0 - 2.0 ** (-5.0 - jnp.arange(RET_HEADS, dtype=jnp.float32)))
    cond = jax.nn.silu(c)
    for i in range(DEPTH):
        mod = cond @ ada_w[i] + ada_b[i]
        sh1, sc1, g1, sh2, sc2, g2 = jnp.split(mod, 6, axis=-1)
        h = modulate(rmsnorm(x, norm_mix_g[i]), sh1, sc1)
        j = i // N_MIXERS
        if i % N_MIXERS == 0:
            y = conformer_conv(h, conv_w_pw1[j], conv_b_pw1[j], conv_w_dw[j], conv_b_dw[j],
                               conv_ln_g[j], conv_ln_b[j], conv_w_pw2[j], conv_b_pw2[j])
        else:
            y = retention(h, ret_w_in[j], ret_gn_g[j], ret_gn_b[j], ret_w_out[j],
                          cos, sin, log_gamma)
        x = x + g1[:, None, :] * y
        h = modulate(rmsnorm(x, norm_mlp_g[i]), sh2, sc2)
        x = x + g2[:, None, :] * (jnp.square(jax.nn.relu(h @ mlp_w1[i])) @ mlp_w2[i])
    return rmsnorm(x, final_norm_g)
```

```python
import functools

import numpy as np
import jax
import jax.numpy as jnp
from jax import lax
from jax.experimental import pallas as pl
from jax.experimental.pallas import tpu as pltpu

F32 = jnp.float32
BF16 = jnp.bfloat16

EPS = 1e-6
CHUNK = 64
CONV_WIDTH = 31
RET_HEADS = 4
ROPE_BASE = 10000.0

V7X_SUBLANES = 8
V7X_LANES = 128
V7X_VMEM_LIMIT_BYTES = 56 * 1024 * 1024

CONV_TM = 512
CONV_HALO = 32
CONV_ROWS = 64
MLP_TM = 512
MLP_FF_CHUNK = 1024
RET_L = 256


def _resident(shape):
    zeros = (0,) * len(shape)
    return pl.BlockSpec(shape, lambda *_: zeros, pipeline_mode=pl.Buffered(1))


def _rmsnorm_mod(x, g, shift, scale):
    y = x * lax.rsqrt(jnp.mean(x * x, axis=-1, keepdims=True) + EPS)
    return (y * g) * (1.0 + scale) + shift


def _silu(v):
    return v * jax.nn.sigmoid(v)


def _ada_kernel(c_ref, w_ref, b_ref, o_ref):
    cond = _silu(c_ref[...])
    o_ref[...] = jnp.dot(cond.astype(BF16), w_ref[...].astype(BF16),
                         preferred_element_type=F32) + b_ref[...]


def _ada(c, ada_w, ada_b):
    depth, d, n = ada_w.shape
    b = c.shape[0]
    tn = n // 4
    return pl.pallas_call(
        _ada_kernel,
        out_shape=jax.ShapeDtypeStruct((depth, b, n), F32),
        grid=(depth, n // tn),
        in_specs=[
            pl.BlockSpec((b, d), lambda l, j: (0, 0)),
            pl.BlockSpec((None, d, tn), lambda l, j: (l, 0, j)),
            pl.BlockSpec((None, 1, tn), lambda l, j: (l, 0, j)),
        ],
        out_specs=pl.BlockSpec((None, b, tn), lambda l, j: (l, 0, j)),
        compiler_params=pltpu.CompilerParams(
            dimension_semantics=("arbitrary", "arbitrary"),
            vmem_limit_bytes=V7X_VMEM_LIMIT_BYTES),
        name="ada_mod",
    )(c, ada_w, ada_b.reshape(depth, 1, n))


def _conv_kernel(x_ref, mod_ref, g_ref, w1_ref, b1_ref, wdw_ref, bdw_ref, lng_ref, lnb_ref,
                 w2_ref, b2_ref, o_ref, ubuf, cbuf, vbuf):
    tm, d = x_ref.shape
    groups = d // V7X_LANES
    j = pl.program_id(1)

    @pl.when(j == 0)
    def _():
        ubuf[:, 0:CONV_HALO, :] = jnp.zeros((groups, CONV_HALO, V7X_LANES), F32)

    @pl.when(j > 0)
    def _():
        ubuf[:, 0:CONV_HALO, :] = ubuf[:, tm:tm + CONV_HALO, :]

    x = x_ref[...]
    h = _rmsnorm_mod(x, g_ref[...], mod_ref[0:1, :], mod_ref[1:2, :]).astype(BF16)
    a = jnp.dot(h, w1_ref[:, 0:d], preferred_element_type=F32) + b1_ref[:, 0:d]
    gt = jnp.dot(h, w1_ref[:, d:2 * d], preferred_element_type=F32) + b1_ref[:, d:2 * d]
    u = a * jax.nn.sigmoid(gt)
    for gi in range(groups):
        ubuf[gi, CONV_HALO:CONV_HALO + tm, :] = u[:, gi * V7X_LANES:(gi + 1) * V7X_LANES]

    first_tap = CONV_HALO - (CONV_WIDTH - 1)

    def conv_rows(i, carry):
        r0 = pl.multiple_of(i * CONV_ROWS, CONV_ROWS)
        for gi in range(groups):
            lanes = slice(gi * V7X_LANES, (gi + 1) * V7X_LANES)
            win = ubuf.at[gi, pl.ds(r0, CONV_ROWS + CONV_HALO), :]
            acc = jnp.zeros((CONV_ROWS, V7X_LANES), F32) + bdw_ref[:, lanes]
            for k in range(CONV_WIDTH):
                acc = acc + win[pl.ds(first_tap + k, CONV_ROWS, stride=1), :] * wdw_ref[k:k + 1, lanes]
            cbuf[pl.ds(r0, CONV_ROWS), lanes] = acc
        cv = cbuf[pl.ds(r0, CONV_ROWS), :]
        mu = jnp.mean(cv, axis=-1, keepdims=True)
        cen = cv - mu
        var = jnp.mean(cen * cen, axis=-1, keepdims=True)
        v = cen * lax.rsqrt(var + EPS) * lng_ref[...] + lnb_ref[...]
        vbuf[pl.ds(r0, CONV_ROWS), :] = _silu(v).astype(BF16)
        return carry

    lax.fori_loop(0, tm // CONV_ROWS, conv_rows, 0)

    y = jnp.dot(vbuf[...], w2_ref[...], preferred_element_type=F32) + b2_ref[...]
    o_ref[...] = x + mod_ref[2:3, :] * y


def _conv_layer(x, mod, g, w1, b1, wdw, bdw, lng, lnb, w2, b2):
    b, s, d = x.shape
    tm = CONV_TM
    row = lambda v: v.reshape(1, -1)
    return pl.pallas_call(
        _conv_kernel,
        out_shape=jax.ShapeDtypeStruct(x.shape, F32),
        grid=(b, s // tm),
        in_specs=[
            pl.BlockSpec((None, tm, d), lambda bi, j: (bi, j, 0)),
            pl.BlockSpec((None, 6, d), lambda bi, j: (bi, 0, 0)),
            _resident((1, d)),
            _resident((d, 2 * d)),
            _resident((1, 2 * d)),
            _resident((CONV_WIDTH, d)),
            _resident((1, d)),
            _resident((1, d)),
            _resident((1, d)),
            _resident((d, d)),
            _resident((1, d)),
        ],
        out_specs=pl.BlockSpec((None, tm, d), lambda bi, j: (bi, j, 0)),
        scratch_shapes=[
            pltpu.VMEM((d // V7X_LANES, CONV_HALO + tm, V7X_LANES), F32),
            pltpu.VMEM((tm, d), F32),
            pltpu.VMEM((tm, d), BF16),
        ],
        compiler_params=pltpu.CompilerParams(
            dimension_semantics=("arbitrary", "arbitrary"),
            vmem_limit_bytes=V7X_VMEM_LIMIT_BYTES),
        name="conv_mixer",
    )(x, mod, row(g), w1, row(b1), wdw, row(bdw), row(lng), row(lnb), w2, row(b2))


def _mlp_kernel(x_ref, mod_ref, g_ref, w1_ref, w2_ref, fg_ref, o_ref, *, final):
    d_ff = w1_ref.shape[1]
    x = x_ref[...]
    h = _rmsnorm_mod(x, g_ref[...], mod_ref[3:4, :], mod_ref[4:5, :]).astype(BF16)
    acc = jnp.zeros(x.shape, F32)
    for c0 in range(0, d_ff, MLP_FF_CHUNK):
        t = jnp.dot(h, w1_ref[:, c0:c0 + MLP_FF_CHUNK], preferred_element_type=F32)
        t = jnp.maximum(t, 0.0)
        acc = acc + jnp.dot((t * t).astype(BF16), w2_ref[c0:c0 + MLP_FF_CHUNK, :],
                            preferred_element_type=F32)
    y = x + mod_ref[5:6, :] * acc
    if final:
        y = y * lax.rsqrt(jnp.mean(y * y, axis=-1, keepdims=True) + EPS) * fg_ref[...]
    o_ref[...] = y


def _mlp_layer(x, mod, g, w1, w2, fg, final):
    b, s, d = x.shape
    d_ff = w1.shape[1]
    tm = MLP_TM
    return pl.pallas_call(
        functools.partial(_mlp_kernel, final=final),
        out_shape=jax.ShapeDtypeStruct(x.shape, F32),
        grid=(b, s // tm),
        in_specs=[
            pl.BlockSpec((None, tm, d), lambda bi, j: (bi, j, 0)),
            pl.BlockSpec((None, 6, d), lambda bi, j: (bi, 0, 0)),
            _resident((1, d)),
            _resident((d, d_ff)),
            _resident((d_ff, d)),
            _resident((1, d)),
        ],
        out_specs=pl.BlockSpec((None, tm, d), lambda bi, j: (bi, j, 0)),
        compiler_params=pltpu.CompilerParams(
            dimension_semantics=("arbitrary", "arbitrary"),
            vmem_limit_bytes=V7X_VMEM_LIMIT_BYTES),
        name="mlp_final" if final else "mlp",
    )(x, mod, g.reshape(1, d), w1, w2, fg.reshape(1, d))


def _log_gamma(head):
    return float(np.log(np.float32(1.0) - np.float32(2.0) ** np.float32(-5.0 - head)))


def _ret_kernel(x_ref, mod_ref, g_ref, cos_ref, sin_ref, win_ref, gng_ref, gnb_ref, wout_ref,
                o_ref, proj, ybuf, state, dmask):
    L, d = x_ref.shape
    heads = RET_HEADS
    dk = d // heads
    dv = 2 * d // heads
    half = dk // 2
    j = pl.program_id(1)

    @pl.when(j == 0)
    def _():
        state[...] = jnp.zeros(state.shape, F32)
        n = lax.broadcasted_iota(jnp.int32, (L, L), 0)
        m = lax.broadcasted_iota(jnp.int32, (L, L), 1)
        dist = jnp.abs(n - m).astype(F32)
        chunk_shift = CHUNK.bit_length() - 1
        visible = jnp.right_shift(m, chunk_shift) <= jnp.right_shift(n, chunk_shift)
        for hd in range(heads):
            dmask[hd] = jnp.where(visible, jnp.exp(_log_gamma(hd) * dist), 0.0)

    x = x_ref[...]
    h = _rmsnorm_mod(x, g_ref[...], mod_ref[0:1, :], mod_ref[1:2, :]).astype(BF16)
    n_in = win_ref.shape[1]
    for c0 in range(0, n_in, d):
        proj[:, c0:c0 + d] = jnp.dot(h, win_ref[:, c0:c0 + d], preferred_element_type=F32)

    cos = cos_ref[...]
    sin = sin_ref[...]
    idx = lax.broadcasted_iota(jnp.int32, (L, 1), 0).astype(F32)
    k_off, v_off, g_off = d, 2 * d, 2 * d + heads * dv

    def rope(base):
        x1 = proj[:, base:base + half]
        x2 = proj[:, base + half:base + dk]
        return jnp.concatenate([x1 * cos - x2 * sin, x2 * cos + x1 * sin], axis=-1)

    for hd in range(heads):
        lg = _log_gamma(hd)
        q = rope(hd * dk)
        k = rope(k_off + hd * dk) * (dk ** -0.5)
        vb = proj[:, v_off + hd * dv:v_off + (hd + 1) * dv].astype(BF16)
        qb = q.astype(BF16)
        scores = lax.dot_general(qb, k.astype(BF16), (((1,), (1,)), ((), ())),
                                 preferred_element_type=F32) * dmask[hd]
        intra = jnp.dot(scores.astype(BF16), vb, preferred_element_type=F32)
        xi = jnp.exp(lg * (idx + 1.0))
        st = state[hd]
        cross = jnp.dot(qb, st.astype(BF16), preferred_element_type=F32) * xi
        zeta = jnp.exp(lg * (float(L - 1) - idx))
        kz = (k * zeta).astype(BF16)
        state[hd] = st * float(np.exp(np.float32(lg) * np.float32(L))) + lax.dot_general(
            kz, vb, (((0,), (0,)), ((), ())), preferred_element_type=F32)
        y = intra + cross
        mu = jnp.mean(y, axis=-1, keepdims=True)
        cen = y - mu
        var = jnp.mean(cen * cen, axis=-1, keepdims=True)
        yn = cen * lax.rsqrt(var + EPS) * gng_ref[hd:hd + 1, :] + gnb_ref[hd:hd + 1, :]
        gate = proj[:, g_off + hd * dv:g_off + (hd + 1) * dv]
        ybuf[:, hd * dv:(hd + 1) * dv] = (_silu(gate) * yn).astype(BF16)

    out = jnp.dot(ybuf[...], wout_ref[...], preferred_element_type=F32)
    o_ref[...] = x + mod_ref[2:3, :] * out


def _ret_layer(x, mod, g, cos, sin, w_in, gn_g, gn_b, w_out):
    b, s, d = x.shape
    L = RET_L
    heads = RET_HEADS
    dk, dv = d // heads, 2 * d // heads
    n_in = w_in.shape[1]
    return pl.pallas_call(
        _ret_kernel,
        out_shape=jax.ShapeDtypeStruct(x.shape, F32),
        grid=(b, s // L),
        in_specs=[
            pl.BlockSpec((None, L, d), lambda bi, j: (bi, j, 0)),
            pl.BlockSpec((None, 6, d), lambda bi, j: (bi, 0, 0)),
            _resident((1, d)),
            pl.BlockSpec((L, dk // 2), lambda bi, j: (j, 0)),
            pl.BlockSpec((L, dk // 2), lambda bi, j: (j, 0)),
            _resident((d, n_in)),
            _resident((heads, dv)),
            _resident((heads, dv)),
            _resident((heads * dv, d)),
        ],
        out_specs=pl.BlockSpec((None, L, d), lambda bi, j: (bi, j, 0)),
        scratch_shapes=[
            pltpu.VMEM((L, n_in), F32),
            pltpu.VMEM((L, heads * dv), BF16),
            pltpu.VMEM((heads, dk, dv), F32),
            pltpu.VMEM((heads, L, L), F32),
        ],
        compiler_params=pltpu.CompilerParams(
            dimension_semantics=("arbitrary", "arbitrary"),
            vmem_limit_bytes=V7X_VMEM_LIMIT_BYTES),
        name="retention_mixer",
    )(x, mod, g.reshape(1, d), cos, sin, w_in, gn_g, gn_b, w_out)


def _rope_tables(seq, dk):
    pos = jnp.arange(seq, dtype=F32)
    inv = ROPE_BASE ** (-jnp.arange(0, dk, 2, dtype=F32) / dk)
    ang = pos[:, None] * inv[None, :]
    return jnp.cos(ang), jnp.sin(ang)


def kernel(x, c, ada_w, ada_b, norm_mix_g, norm_mlp_g, conv_w_pw1, conv_b_pw1, conv_w_dw, conv_b_dw, conv_ln_g, conv_ln_b, conv_w_pw2, conv_b_pw2, ret_w_in, ret_gn_g, ret_gn_b, ret_w_out, mlp_w1, mlp_w2, final_norm_g):
    depth = ada_w.shape[0]
    b, s, d = x.shape
    assert s % CONV_TM == 0 and s % MLP_TM == 0 and s % RET_L == 0 and RET_L % CHUNK == 0
    assert CONV_HALO >= CONV_WIDTH - 1 and CONV_HALO % V7X_SUBLANES == 0

    mod = _ada(c, ada_w, ada_b).reshape(depth, b, 6, d)
    cos, sin = _rope_tables(s, d // RET_HEADS)
    for i in range(depth):
        jm = i // 2
        if i % 2 == 0:
            x = _conv_layer(x, mod[i], norm_mix_g[i], conv_w_pw1[jm].astype(BF16), conv_b_pw1[jm],
                            conv_w_dw[jm], conv_b_dw[jm], conv_ln_g[jm], conv_ln_b[jm],
                            conv_w_pw2[jm].astype(BF16), conv_b_pw2[jm])
        else:
            x = _ret_layer(x, mod[i], norm_mix_g[i], cos, sin, ret_w_in[jm].astype(BF16),
                           ret_gn_g[jm], ret_gn_b[jm], ret_w_out[jm].astype(BF16))
        x = _mlp_layer(x, mod[i], norm_mlp_g[i], mlp_w1[i].astype(BF16), mlp_w2[i].astype(BF16),
                       final_norm_g, final=(i == depth - 1))
    return x
```

```python
import functools

import numpy as np
import jax
import jax.numpy as jnp
from jax import lax
from jax.experimental import pallas as pl
from jax.experimental.pallas import tpu as pltpu

F32 = jnp.float32
BF16 = jnp.bfloat16

EPS = 1e-6
CHUNK = 64
CONV_WIDTH = 31
RET_HEADS = 4
ROPE_BASE = 10000.0

V7X_SUBLANES = 8
V7X_LANES = 128
V7X_MXU_COLS = 256
V7X_VMEM_LIMIT_BYTES = 56 * 1024 * 1024

CONV_TM = 512
CONV_HALO = 32
CONV_ROWS = 32
MLP_PHASES = 4
MLP_TM = 512
MLP_FF_CHUNK = 1024
RET_L = 256


def _resident(shape):
    zeros = (0,) * len(shape)
    return pl.BlockSpec(shape, lambda *_: zeros, pipeline_mode=pl.Buffered(1))


def _rmsnorm_mod(x, g, shift, scale):
    y = x * lax.rsqrt(jnp.mean(x * x, axis=-1, keepdims=True) + EPS)
    return (y * g) * (1.0 + scale) + shift


def _silu(v):
    return v * jax.nn.sigmoid(v)


def _after(value, dep):
    bits = pltpu.bitcast(dep, jnp.uint32)
    zero = lax.shift_right_logical(lax.shift_right_logical(bits, jnp.uint32(16)), jnp.uint32(16))
    return pltpu.bitcast(pltpu.bitcast(value, jnp.uint32) + zero, F32)


def _ada_kernel(c_ref, w_ref, b_ref, o_ref):
    cond = _silu(c_ref[...])
    o_ref[...] = jnp.dot(cond.astype(BF16), w_ref[...].astype(BF16),
                         preferred_element_type=F32) + b_ref[...]


def _ada(c, ada_w, ada_b):
    depth, d, n = ada_w.shape
    b = c.shape[0]
    tn = n // 4
    return pl.pallas_call(
        _ada_kernel,
        out_shape=jax.ShapeDtypeStruct((depth, b, n), F32),
        grid=(depth, n // tn),
        in_specs=[
            pl.BlockSpec((b, d), lambda l, j: (0, 0)),
            pl.BlockSpec((None, d, tn), lambda l, j: (l, 0, j)),
            pl.BlockSpec((None, 1, tn), lambda l, j: (l, 0, j)),
        ],
        out_specs=pl.BlockSpec((None, b, tn), lambda l, j: (l, 0, j)),
        compiler_params=pltpu.CompilerParams(
            dimension_semantics=("arbitrary", "arbitrary"),
            vmem_limit_bytes=V7X_VMEM_LIMIT_BYTES),
        name="ada_mod",
    )(c, ada_w, ada_b.reshape(depth, 1, n))


def _conv_mlp_kernel(x_ref, modc_ref, modp_ref, gmix_ref, w1_ref, b1_ref, wdw_ref, bdw_ref,
                     lng_ref, lnb_ref, w2_ref, b2_ref, gmlp_ref, m1_ref, m2_ref, o_ref,
                     ubuf, cbuf, x1buf, h2buf, tbuf, fbuf, *, tiles_per_seq):
    tm, d = x_ref.shape
    groups = d // V7X_LANES
    fw = m1_ref.shape[2]
    n = pl.program_id(0)

    @pl.when(n == 0)
    def _():
        x1buf[...] = jnp.zeros(x1buf.shape, F32)
        h2buf[...] = jnp.zeros(h2buf.shape, BF16)

    @pl.when(n % tiles_per_seq == 0)
    def _():
        ubuf[:, 0:CONV_HALO, :] = jnp.zeros((groups, CONV_HALO, V7X_LANES), F32)

    @pl.when(n % tiles_per_seq != 0)
    def _():
        ubuf[:, 0:CONV_HALO, :] = ubuf[:, tm:tm + CONV_HALO, :]

    h = _rmsnorm_mod(x_ref[...], gmix_ref[...], modc_ref[0:1, :], modc_ref[1:2, :]).astype(BF16)
    a = jnp.dot(h, w1_ref[:, 0:d], preferred_element_type=F32) + b1_ref[:, 0:d]
    gt = jnp.dot(h, w1_ref[:, d:2 * d], preferred_element_type=F32) + b1_ref[:, d:2 * d]
    u = a * jax.nn.sigmoid(gt)
    for gi in range(groups):
        ubuf[gi, CONV_HALO:CONV_HALO + tm, :] = u[:, gi * V7X_LANES:(gi + 1) * V7X_LANES]

    first_tap = CONV_HALO - (CONV_WIDTH - 1)

    def conv_taps(gi, chains=1):
        done = None
        for r0 in range(0, tm, CONV_ROWS):
            if r0 % (tm // chains) == 0:
                done = None
            win = ubuf.at[gi, r0:r0 + CONV_ROWS + CONV_HALO, :]
            tap = jnp.zeros((CONV_ROWS, V7X_LANES), F32) + bdw_ref[gi]
            if done is not None:
                tap = _after(tap, done)
            for t in range(CONV_WIDTH):
                tap = tap + win[pl.ds(first_tap + t, CONV_ROWS, stride=1), :] * wdw_ref[gi, t:t + 1, :]
            cbuf[gi, r0:r0 + CONV_ROWS, :] = tap
            done = tap

    def hidden_phase(p, carry):
        t1 = jnp.dot(h2buf[...], m1_ref[p], preferred_element_type=F32)
        t1 = jnp.maximum(t1, 0.0)
        tbuf[p] = (t1 * t1).astype(BF16)
        conv_taps(p, chains=4)
        return carry

    lax.fori_loop(0, MLP_PHASES, hidden_phase, 0)

    def out_phase(p, carry):
        ff = jnp.dot(tbuf[0], m2_ref[p, 0:fw, :], preferred_element_type=F32)
        for q in range(1, MLP_PHASES):
            ff = ff + jnp.dot(tbuf[q], m2_ref[p, q * fw:(q + 1) * fw, :], preferred_element_type=F32)
        conv_taps(MLP_PHASES + p)
        fbuf[p] = ff
        return carry

    lax.fori_loop(0, MLP_PHASES, out_phase, 0)

    ff = jnp.concatenate([fbuf[p] for p in range(MLP_PHASES)], axis=-1)
    o_ref[...] = x1buf[...] + modp_ref[5:6, :] * ff

    cv = jnp.concatenate([cbuf[gi] for gi in range(groups)], axis=-1)
    mu = jnp.mean(cv, axis=-1, keepdims=True)
    cen = cv - mu
    var = jnp.mean(cen * cen, axis=-1, keepdims=True)
    v = cen * lax.rsqrt(var + EPS) * lng_ref[...] + lnb_ref[...]
    y = jnp.dot(_silu(v).astype(BF16), w2_ref[...], preferred_element_type=F32) + b2_ref[...]
    x1 = x_ref[...] + modc_ref[2:3, :] * y
    x1buf[...] = x1
    h2buf[...] = _rmsnorm_mod(x1, gmlp_ref[...], modc_ref[3:4, :], modc_ref[4:5, :]).astype(BF16)


def _conv_mlp_layer(x, mod, gmix, w1, b1, wdw, bdw, lng, lnb, w2, b2, gmlp, m1, m2):
    b, s, d = x.shape
    d_ff = m1.shape[1]
    tm = CONV_TM
    tiles_per_seq = s // tm
    n_tiles = b * tiles_per_seq
    groups = d // V7X_LANES
    fw = d_ff // MLP_PHASES
    ow = d // MLP_PHASES
    row = lambda v: v.reshape(1, -1)
    cur = lambda n: jnp.minimum(n, n_tiles - 1)
    prev = lambda n: jnp.maximum(n - 1, 0)
    m1_blocks = m1.reshape(d, MLP_PHASES, fw).transpose(1, 0, 2)
    m2_blocks = m2.reshape(d_ff, MLP_PHASES, ow).transpose(1, 0, 2)
    out = pl.pallas_call(
        functools.partial(_conv_mlp_kernel, tiles_per_seq=tiles_per_seq),
        out_shape=jax.ShapeDtypeStruct((b * s, d), F32),
        grid=(n_tiles + 1,),
        in_specs=[
            pl.BlockSpec((tm, d), lambda n: (cur(n), 0)),
            pl.BlockSpec((None, 6, d), lambda n: (cur(n) // tiles_per_seq, 0, 0)),
            pl.BlockSpec((None, 6, d), lambda n: (prev(n) // tiles_per_seq, 0, 0)),
            _resident((1, d)),
            _resident((d, 2 * d)),
            _resident((1, 2 * d)),
            _resident((groups, CONV_WIDTH, V7X_LANES)),
            _resident((groups, 1, V7X_LANES)),
            _resident((1, d)),
            _resident((1, d)),
            _resident((d, d)),
            _resident((1, d)),
            _resident((1, d)),
            _resident((MLP_PHASES, d, fw)),
            _resident((MLP_PHASES, d_ff, ow)),
        ],
        out_specs=pl.BlockSpec((tm, d), lambda n: (prev(n), 0)),
        scratch_shapes=[
            pltpu.VMEM((groups, CONV_HALO + tm, V7X_LANES), F32),
            pltpu.VMEM((groups, tm, V7X_LANES), F32),
            pltpu.VMEM((tm, d), F32),
            pltpu.VMEM((tm, d), BF16),
            pltpu.VMEM((MLP_PHASES, tm, fw), BF16),
            pltpu.VMEM((MLP_PHASES, tm, ow), F32),
        ],
        compiler_params=pltpu.CompilerParams(
            dimension_semantics=("arbitrary",),
            vmem_limit_bytes=V7X_VMEM_LIMIT_BYTES),
        name="conv_mlp",
    )(x.reshape(b * s, d), mod, mod, row(gmix), w1, row(b1),
      wdw.reshape(CONV_WIDTH, groups, V7X_LANES).transpose(1, 0, 2), bdw.reshape(groups, 1, V7X_LANES),
      row(lng), row(lnb), w2, row(b2), row(gmlp), m1_blocks, m2_blocks)
    return out.reshape(b, s, d)


def _mlp_kernel(x_ref, mod_ref, g_ref, w1_ref, w2_ref, fg_ref, o_ref, *, final):
    d_ff = w1_ref.shape[1]
    x = x_ref[...]
    h = _rmsnorm_mod(x, g_ref[...], mod_ref[3:4, :], mod_ref[4:5, :]).astype(BF16)
    acc = jnp.zeros(x.shape, F32)
    for c0 in range(0, d_ff, MLP_FF_CHUNK):
        t = jnp.dot(h, w1_ref[:, c0:c0 + MLP_FF_CHUNK], preferred_element_type=F32)
        t = jnp.maximum(t, 0.0)
        acc = acc + jnp.dot((t * t).astype(BF16), w2_ref[c0:c0 + MLP_FF_CHUNK, :],
                            preferred_element_type=F32)
    y = x + mod_ref[5:6, :] * acc
    if final:
        y = y * lax.rsqrt(jnp.mean(y * y, axis=-1, keepdims=True) + EPS) * fg_ref[...]
    o_ref[...] = y


def _mlp_layer(x, mod, g, w1, w2, fg, final):
    b, s, d = x.shape
    d_ff = w1.shape[1]
    tm = MLP_TM
    return pl.pallas_call(
        functools.partial(_mlp_kernel, final=final),
        out_shape=jax.ShapeDtypeStruct(x.shape, F32),
        grid=(b, s // tm),
        in_specs=[
            pl.BlockSpec((None, tm, d), lambda bi, j: (bi, j, 0)),
            pl.BlockSpec((None, 6, d), lambda bi, j: (bi, 0, 0)),
            _resident((1, d)),
            _resident((d, d_ff)),
            _resident((d_ff, d)),
            _resident((1, d)),
        ],
        out_specs=pl.BlockSpec((None, tm, d), lambda bi, j: (bi, j, 0)),
        compiler_params=pltpu.CompilerParams(
            dimension_semantics=("arbitrary", "arbitrary"),
            vmem_limit_bytes=V7X_VMEM_LIMIT_BYTES),
        name="mlp_final" if final else "mlp",
    )(x, mod, g.reshape(1, d), w1, w2, fg.reshape(1, d))


def _log_gamma(head):
    return float(np.log(np.float32(1.0) - np.float32(2.0) ** np.float32(-5.0 - head)))


def _ret_kernel(x_ref, mod_ref, g_ref, cos_ref, sin_ref, win_ref, gng_ref, gnb_ref, wout_ref,
                o_ref, proj, ybuf, state, dmask):
    L, d = x_ref.shape
    heads = RET_HEADS
    dk = d // heads
    dv = 2 * d // heads
    half = dk // 2
    j = pl.program_id(1)

    @pl.when(j == 0)
    def _():
        state[...] = jnp.zeros(state.shape, F32)
        n = lax.broadcasted_iota(jnp.int32, (L, L), 0)
        m = lax.broadcasted_iota(jnp.int32, (L, L), 1)
        dist = jnp.abs(n - m).astype(F32)
        chunk_shift = CHUNK.bit_length() - 1
        visible = jnp.right_shift(m, chunk_shift) <= jnp.right_shift(n, chunk_shift)
        for hd in range(heads):
            dmask[hd] = jnp.where(visible, jnp.exp(_log_gamma(hd) * dist), 0.0)

    x = x_ref[...]
    h = _rmsnorm_mod(x, g_ref[...], mod_ref[0:1, :], mod_ref[1:2, :]).astype(BF16)
    n_in = win_ref.shape[1]
    for c0 in range(0, n_in, d):
        proj[:, c0:c0 + d] = jnp.dot(h, win_ref[:, c0:c0 + d], preferred_element_type=F32)

    cos = cos_ref[...]
    sin = sin_ref[...]
    idx = lax.broadcasted_iota(jnp.int32, (L, 1), 0).astype(F32)
    k_off, v_off, g_off = d, 2 * d, 2 * d + heads * dv

    def rope(base):
        x1 = proj[:, base:base + half]
        x2 = proj[:, base + half:base + dk]
        return jnp.concatenate([x1 * cos - x2 * sin, x2 * cos + x1 * sin], axis=-1)

    for hd in range(heads):
        lg = _log_gamma(hd)
        q = rope(hd * dk)
        k = rope(k_off + hd * dk) * (dk ** -0.5)
        vb = proj[:, v_off + hd * dv:v_off + (hd + 1) * dv].astype(BF16)
        qb = q.astype(BF16)
        scores = lax.dot_general(qb, k.astype(BF16), (((1,), (1,)), ((), ())),
                                 preferred_element_type=F32) * dmask[hd]
        intra = jnp.dot(scores.astype(BF16), vb, preferred_element_type=F32)
        xi = jnp.exp(lg * (idx + 1.0))
        st = state[hd]
        cross = jnp.dot(qb, st.astype(BF16), preferred_element_type=F32) * xi
        zeta = jnp.exp(lg * (float(L - 1) - idx))
        kz = (k * zeta).astype(BF16)
        state[hd] = st * float(np.exp(np.float32(lg) * np.float32(L))) + lax.dot_general(
            kz, vb, (((0,), (0,)), ((), ())), preferred_element_type=F32)
        y = intra + cross
        mu = jnp.mean(y, axis=-1, keepdims=True)
        cen = y - mu
        var = jnp.mean(cen * cen, axis=-1, keepdims=True)
        yn = cen * lax.rsqrt(var + EPS) * gng_ref[hd:hd + 1, :] + gnb_ref[hd:hd + 1, :]
        gate = proj[:, g_off + hd * dv:g_off + (hd + 1) * dv]
        ybuf[:, hd * dv:(hd + 1) * dv] = (_silu(gate) * yn).astype(BF16)

    out = jnp.dot(ybuf[...], wout_ref[...], preferred_element_type=F32)
    o_ref[...] = x + mod_ref[2:3, :] * out


def _ret_layer(x, mod, g, cos, sin, w_in, gn_g, gn_b, w_out):
    b, s, d = x.shape
    L = RET_L
    heads = RET_HEADS
    dk, dv = d // heads, 2 * d // heads
    n_in = w_in.shape[1]
    return pl.pallas_call(
        _ret_kernel,
        out_shape=jax.ShapeDtypeStruct(x.shape, F32),
        grid=(b, s // L),
        in_specs=[
            pl.BlockSpec((None, L, d), lambda bi, j: (bi, j, 0)),
            pl.BlockSpec((None, 6, d), lambda bi, j: (bi, 0, 0)),
            _resident((1, d)),
            pl.BlockSpec((L, dk // 2), lambda bi, j: (j, 0)),
            pl.BlockSpec((L, dk // 2), lambda bi, j: (j, 0)),
            _resident((d, n_in)),
            _resident((heads, dv)),
            _resident((heads, dv)),
            _resident((heads * dv, d)),
        ],
        out_specs=pl.BlockSpec((None, L, d), lambda bi, j: (bi, j, 0)),
        scratch_shapes=[
            pltpu.VMEM((L, n_in), F32),
            pltpu.VMEM((L, heads * dv), BF16),
            pltpu.VMEM((heads, dk, dv), F32),
            pltpu.VMEM((heads, L, L), F32),
        ],
        compiler_params=pltpu.CompilerParams(
            dimension_semantics=("arbitrary", "arbitrary"),
            vmem_limit_bytes=V7X_VMEM_LIMIT_BYTES),
        name="retention_mixer",
    )(x, mod, g.reshape(1, d), cos, sin, w_in, gn_g, gn_b, w_out)


def _rope_tables(seq, dk):
    pos = jnp.arange(seq, dtype=F32)
    inv = ROPE_BASE ** (-jnp.arange(0, dk, 2, dtype=F32) / dk)
    ang = pos[:, None] * inv[None, :]
    return jnp.cos(ang), jnp.sin(ang)


def kernel(x, c, ada_w, ada_b, norm_mix_g, norm_mlp_g, conv_w_pw1, conv_b_pw1, conv_w_dw, conv_b_dw, conv_ln_g, conv_ln_b, conv_w_pw2, conv_b_pw2, ret_w_in, ret_gn_g, ret_gn_b, ret_w_out, mlp_w1, mlp_w2, final_norm_g):
    depth = ada_w.shape[0]
    b, s, d = x.shape
    assert s % CONV_TM == 0 and s % MLP_TM == 0 and s % RET_L == 0 and RET_L % CHUNK == 0
    assert CONV_HALO >= CONV_WIDTH - 1 and CONV_HALO % V7X_SUBLANES == 0 and CONV_TM % CONV_ROWS == 0
    assert d == 2 * MLP_PHASES * V7X_LANES

    mod = _ada(c, ada_w, ada_b).reshape(depth, b, 6, d)
    cos, sin = _rope_tables(s, d // RET_HEADS)
    for i in range(depth):
        jm = i // 2
        m1, m2 = mlp_w1[i].astype(BF16), mlp_w2[i].astype(BF16)
        if i % 2 == 0:
            assert i != depth - 1
            x = _conv_mlp_layer(x, mod[i], norm_mix_g[i], conv_w_pw1[jm].astype(BF16), conv_b_pw1[jm],
                                conv_w_dw[jm], conv_b_dw[jm], conv_ln_g[jm], conv_ln_b[jm],
                                conv_w_pw2[jm].astype(BF16), conv_b_pw2[jm], norm_mlp_g[i], m1, m2)
        else:
            x = _ret_layer(x, mod[i], norm_mix_g[i], cos, sin, ret_w_in[jm].astype(BF16),
                           ret_gn_g[jm], ret_gn_b[jm], ret_w_out[jm].astype(BF16))
            x = _mlp_layer(x, mod[i], norm_mlp_g[i], m1, m2, final_norm_g, final=(i == depth - 1))
    return x
```

```python
import functools

import numpy as np
import jax
import jax.numpy as jnp
from jax import lax
from jax.experimental import pallas as pl
from jax.experimental.pallas import tpu as pltpu

F32 = jnp.float32
BF16 = jnp.bfloat16

EPS = 1e-6
CHUNK = 64
CONV_WIDTH = 31
RET_HEADS = 4
ROPE_BASE = 10000.0

V7X_SUBLANES = 8
V7X_LANES = 128
V7X_MXU_COLS = 256
V7X_VMEM_LIMIT_BYTES = 56 * 1024 * 1024

CONV_TM = 512
CONV_HALO = 32
CONV_ROWS = 32
MLP_PHASES = 4
MLP_TM = 512
RET_L = 256


def _resident(shape, layer=None):
    zeros = (0,) * len(shape)
    if layer is None:
        return pl.BlockSpec(shape, lambda *_: zeros, pipeline_mode=pl.Buffered(1))
    return pl.BlockSpec((None,) + tuple(shape), lambda *_: (layer,) + zeros, pipeline_mode=pl.Buffered(1))


def _mlp_weight_blocks(w1, w2):
    layers, d, d_ff = w1.shape
    w1b = w1.reshape(layers, d, MLP_PHASES, d_ff // MLP_PHASES).transpose(0, 2, 1, 3).astype(BF16)
    w2b = w2.reshape(layers, d_ff, MLP_PHASES, d // MLP_PHASES).transpose(0, 2, 1, 3).astype(BF16)
    return w1b, w2b


def _rmsnorm_mod(x, g, shift, scale):
    y = x * lax.rsqrt(jnp.mean(x * x, axis=-1, keepdims=True) + EPS)
    return (y * g) * (1.0 + scale) + shift


def _silu(v):
    return v * jax.nn.sigmoid(v)


def _after(value, dep):
    bits = pltpu.bitcast(dep, jnp.uint32)
    zero = lax.shift_right_logical(lax.shift_right_logical(bits, jnp.uint32(16)), jnp.uint32(16))
    return pltpu.bitcast(pltpu.bitcast(value, jnp.uint32) + zero, F32)


def _ada_kernel(c_ref, w_ref, b_ref, o_ref):
    cond = _silu(c_ref[...])
    o_ref[...] = jnp.dot(cond.astype(BF16), w_ref[...].astype(BF16),
                         preferred_element_type=F32) + b_ref[...]


def _ada(c, ada_w, ada_b):
    depth, d, n = ada_w.shape
    b = c.shape[0]
    tn = n // 4
    return pl.pallas_call(
        _ada_kernel,
        out_shape=jax.ShapeDtypeStruct((depth, b, n), F32),
        grid=(depth, n // tn),
        in_specs=[
            pl.BlockSpec((b, d), lambda l, j: (0, 0)),
            pl.BlockSpec((None, d, tn), lambda l, j: (l, 0, j)),
            pl.BlockSpec((None, 1, tn), lambda l, j: (l, 0, j)),
        ],
        out_specs=pl.BlockSpec((None, b, tn), lambda l, j: (l, 0, j)),
        compiler_params=pltpu.CompilerParams(
            dimension_semantics=("arbitrary", "arbitrary"),
            vmem_limit_bytes=V7X_VMEM_LIMIT_BYTES),
        name="ada_mod",
    )(c, ada_w, ada_b.reshape(depth, 1, n))


def _conv_mlp_kernel(x_ref, modc_ref, modp_ref, gmix_ref, w1_ref, b1_ref, wdw_ref, bdw_ref,
                     lng_ref, lnb_ref, w2_ref, b2_ref, gmlp_ref, m1_ref, m2_ref, o_ref,
                     ubuf, cbuf, x1buf, h2buf, tbuf, fbuf, *, tiles_per_seq):
    tm, d = x_ref.shape
    groups = d // V7X_LANES
    fw = m1_ref.shape[2]
    n = pl.program_id(0)

    @pl.when(n == 0)
    def _():
        x1buf[...] = jnp.zeros(x1buf.shape, F32)
        h2buf[...] = jnp.zeros(h2buf.shape, BF16)

    @pl.when(n % tiles_per_seq == 0)
    def _():
        ubuf[:, 0:CONV_HALO, :] = jnp.zeros((groups, CONV_HALO, V7X_LANES), F32)

    @pl.when(n % tiles_per_seq != 0)
    def _():
        ubuf[:, 0:CONV_HALO, :] = ubuf[:, tm:tm + CONV_HALO, :]

    h = _rmsnorm_mod(x_ref[...], gmix_ref[...], modc_ref[0:1, :], modc_ref[1:2, :]).astype(BF16)
    for c0 in range(0, d, V7X_MXU_COLS):
        a = jnp.dot(h, w1_ref[:, c0:c0 + V7X_MXU_COLS], preferred_element_type=F32) + b1_ref[:, c0:c0 + V7X_MXU_COLS]
        gt = (jnp.dot(h, w1_ref[:, d + c0:d + c0 + V7X_MXU_COLS], preferred_element_type=F32)
              + b1_ref[:, d + c0:d + c0 + V7X_MXU_COLS])
        u = a * jax.nn.sigmoid(gt)
        for k in range(V7X_MXU_COLS // V7X_LANES):
            ubuf[c0 // V7X_LANES + k, CONV_HALO:CONV_HALO + tm, :] = u[:, k * V7X_LANES:(k + 1) * V7X_LANES]

    first_tap = CONV_HALO - (CONV_WIDTH - 1)

    def conv_taps(gi):
        done = None
        for r0 in range(0, tm, CONV_ROWS):
            win = ubuf.at[gi, r0:r0 + CONV_ROWS + CONV_HALO, :]
            tap = jnp.zeros((CONV_ROWS, V7X_LANES), F32) + bdw_ref[gi]
            if done is not None:
                tap = _after(tap, done)
            for t in range(CONV_WIDTH):
                tap = tap + win[pl.ds(first_tap + t, CONV_ROWS, stride=1), :] * wdw_ref[gi, t:t + 1, :]
            cbuf[gi, r0:r0 + CONV_ROWS, :] = tap
            done = tap

    def hidden_phase(p, carry):
        for c0 in range(0, fw, V7X_MXU_COLS):
            t1 = jnp.dot(h2buf[...], m1_ref[p, :, c0:c0 + V7X_MXU_COLS], preferred_element_type=F32)
            t1 = jnp.maximum(t1, 0.0)
            tbuf[p, :, c0:c0 + V7X_MXU_COLS] = (t1 * t1).astype(BF16)
        conv_taps(p)
        return carry

    lax.fori_loop(0, MLP_PHASES, hidden_phase, 0)

    def out_phase(p, carry):
        ff = jnp.dot(tbuf[0], m2_ref[p, 0:fw, :], preferred_element_type=F32)
        for q in range(1, MLP_PHASES):
            ff = ff + jnp.dot(tbuf[q], m2_ref[p, q * fw:(q + 1) * fw, :], preferred_element_type=F32)
        conv_taps(MLP_PHASES + p)
        fbuf[p] = ff
        return carry

    lax.fori_loop(0, MLP_PHASES, out_phase, 0)

    ff = jnp.concatenate([fbuf[p] for p in range(MLP_PHASES)], axis=-1)
    o_ref[...] = x1buf[...] + modp_ref[5:6, :] * ff

    cv = jnp.concatenate([cbuf[gi] for gi in range(groups)], axis=-1)
    mu = jnp.mean(cv, axis=-1, keepdims=True)
    cen = cv - mu
    var = jnp.mean(cen * cen, axis=-1, keepdims=True)
    v = cen * lax.rsqrt(var + EPS) * lng_ref[...] + lnb_ref[...]
    vb = _silu(v).astype(BF16)
    for c0 in range(0, d, V7X_MXU_COLS):
        cols = slice(c0, c0 + V7X_MXU_COLS)
        y = jnp.dot(vb, w2_ref[:, cols], preferred_element_type=F32) + b2_ref[:, cols]
        x1buf[:, cols] = x_ref[:, cols] + modc_ref[2:3, cols] * y
    h2buf[...] = _rmsnorm_mod(x1buf[...], gmlp_ref[...], modc_ref[3:4, :], modc_ref[4:5, :]).astype(BF16)


def _conv_mlp_layer(x, mod, gmix, w1, b1, wdw, bdw, lng, lnb, w2, b2, gmlp, m1_blocks, m2_blocks,
                    conv_layer, mlp_layer):
    b, s, d = x.shape
    fw = m1_blocks.shape[-1]
    ow = m2_blocks.shape[-1]
    d_ff = fw * MLP_PHASES
    tm = CONV_TM
    tiles_per_seq = s // tm
    n_tiles = b * tiles_per_seq
    groups = d // V7X_LANES
    row = lambda v: v.reshape(1, -1)
    cur = lambda n: jnp.minimum(n, n_tiles - 1)
    prev = lambda n: jnp.maximum(n - 1, 0)
    out = pl.pallas_call(
        functools.partial(_conv_mlp_kernel, tiles_per_seq=tiles_per_seq),
        out_shape=jax.ShapeDtypeStruct((b * s, d), F32),
        grid=(n_tiles + 1,),
        in_specs=[
            pl.BlockSpec((tm, d), lambda n: (cur(n), 0)),
            pl.BlockSpec((None, 6, d), lambda n: (cur(n) // tiles_per_seq, 0, 0)),
            pl.BlockSpec((None, 6, d), lambda n: (prev(n) // tiles_per_seq, 0, 0)),
            _resident((1, d)),
            _resident((d, 2 * d), conv_layer),
            _resident((1, 2 * d)),
            _resident((groups, CONV_WIDTH, V7X_LANES)),
            _resident((groups, 1, V7X_LANES)),
            _resident((1, d)),
            _resident((1, d)),
            _resident((d, d), conv_layer),
            _resident((1, d)),
            _resident((1, d)),
            _resident((MLP_PHASES, d, fw), mlp_layer),
            _resident((MLP_PHASES, d_ff, ow), mlp_layer),
        ],
        out_specs=pl.BlockSpec((tm, d), lambda n: (prev(n), 0)),
        scratch_shapes=[
            pltpu.VMEM((groups, CONV_HALO + tm, V7X_LANES), F32),
            pltpu.VMEM((groups, tm, V7X_LANES), F32),
            pltpu.VMEM((tm, d), F32),
            pltpu.VMEM((tm, d), BF16),
            pltpu.VMEM((MLP_PHASES, tm, fw), BF16),
            pltpu.VMEM((MLP_PHASES, tm, ow), F32),
        ],
        compiler_params=pltpu.CompilerParams(
            dimension_semantics=("arbitrary",),
            vmem_limit_bytes=V7X_VMEM_LIMIT_BYTES),
        name="conv_mlp",
    )(x.reshape(b * s, d), mod, mod, row(gmix), w1, row(b1),
      wdw.reshape(CONV_WIDTH, groups, V7X_LANES).transpose(1, 0, 2), bdw.reshape(groups, 1, V7X_LANES),
      row(lng), row(lnb), w2, row(b2), row(gmlp), m1_blocks, m2_blocks)
    return out.reshape(b, s, d)


def _mlp_kernel(x_ref, mod_ref, g_ref, w1_ref, w2_ref, fg_ref, o_ref, *, final):
    fw = w1_ref.shape[2]
    x = x_ref[...]
    h = _rmsnorm_mod(x, g_ref[...], mod_ref[3:4, :], mod_ref[4:5, :]).astype(BF16)
    hidden = []
    for q in range(MLP_PHASES):
        t = jnp.maximum(jnp.dot(h, w1_ref[q], preferred_element_type=F32), 0.0)
        hidden.append((t * t).astype(BF16))
    blocks = []
    for p in range(MLP_PHASES):
        ff = jnp.dot(hidden[0], w2_ref[p, 0:fw, :], preferred_element_type=F32)
        for q in range(1, MLP_PHASES):
            ff = ff + jnp.dot(hidden[q], w2_ref[p, q * fw:(q + 1) * fw, :], preferred_element_type=F32)
        blocks.append(ff)
    y = x + mod_ref[5:6, :] * jnp.concatenate(blocks, axis=-1)
    if final:
        y = y * lax.rsqrt(jnp.mean(y * y, axis=-1, keepdims=True) + EPS) * fg_ref[...]
    o_ref[...] = y


def _mlp_layer(x, mod, g, w1_blocks, w2_blocks, layer, fg, final):
    b, s, d = x.shape
    fw, ow = w1_blocks.shape[-1], w2_blocks.shape[-1]
    d_ff = fw * MLP_PHASES
    tm = MLP_TM
    return pl.pallas_call(
        functools.partial(_mlp_kernel, final=final),
        out_shape=jax.ShapeDtypeStruct(x.shape, F32),
        grid=(b, s // tm),
        in_specs=[
            pl.BlockSpec((None, tm, d), lambda bi, j: (bi, j, 0)),
            pl.BlockSpec((None, 6, d), lambda bi, j: (bi, 0, 0)),
            _resident((1, d)),
            _resident((MLP_PHASES, d, fw), layer),
            _resident((MLP_PHASES, d_ff, ow), layer),
            _resident((1, d)),
        ],
        out_specs=pl.BlockSpec((None, tm, d), lambda bi, j: (bi, j, 0)),
        compiler_params=pltpu.CompilerParams(
            dimension_semantics=("arbitrary", "arbitrary"),
            vmem_limit_bytes=V7X_VMEM_LIMIT_BYTES),
        name="mlp_final" if final else "mlp",
    )(x, mod, g.reshape(1, d), w1_blocks, w2_blocks, fg.reshape(1, d))


def _log_gamma(head):
    return float(np.log(np.float32(1.0) - np.float32(2.0) ** np.float32(-5.0 - head)))


def _ret_kernel(x_ref, mod_ref, g_ref, cos_ref, sin_ref, win_ref, gng_ref, gnb_ref, wout_ref,
                o_ref, proj, ybuf, state, dmask):
    L, d = x_ref.shape
    heads = RET_HEADS
    dk = d // heads
    dv = 2 * d // heads
    half = dk // 2
    j = pl.program_id(1)

    @pl.when(j == 0)
    def _():
        state[...] = jnp.zeros(state.shape, F32)
        n = lax.broadcasted_iota(jnp.int32, (L, L), 0)
        m = lax.broadcasted_iota(jnp.int32, (L, L), 1)
        dist = jnp.abs(n - m).astype(F32)
        chunk_shift = CHUNK.bit_length() - 1
        visible = jnp.right_shift(m, chunk_shift) <= jnp.right_shift(n, chunk_shift)
        for hd in range(heads):
            dmask[hd] = jnp.where(visible, jnp.exp(_log_gamma(hd) * dist), 0.0)

    x = x_ref[...]
    h = _rmsnorm_mod(x, g_ref[...], mod_ref[0:1, :], mod_ref[1:2, :]).astype(BF16)
    n_in = win_ref.shape[1]
    for c0 in range(0, n_in, d):
        proj[:, c0:c0 + d] = jnp.dot(h, win_ref[:, c0:c0 + d], preferred_element_type=F32)

    cos = cos_ref[...]
    sin = sin_ref[...]
    idx = lax.broadcasted_iota(jnp.int32, (L, 1), 0).astype(F32)
    k_off, v_off, g_off = d, 2 * d, 2 * d + heads * dv

    def rope(base):
        x1 = proj[:, base:base + half]
        x2 = proj[:, base + half:base + dk]
        return jnp.concatenate([x1 * cos - x2 * sin, x2 * cos + x1 * sin], axis=-1)

    for hd in range(heads):
        lg = _log_gamma(hd)
        q = rope(hd * dk)
        k = rope(k_off + hd * dk) * (dk ** -0.5)
        vb = proj[:, v_off + hd * dv:v_off + (hd + 1) * dv].astype(BF16)
        qb = q.astype(BF16)
        scores = lax.dot_general(qb, k.astype(BF16), (((1,), (1,)), ((), ())),
                                 preferred_element_type=F32) * dmask[hd]
        intra = jnp.dot(scores.astype(BF16), vb, preferred_element_type=F32)
        xi = jnp.exp(lg * (idx + 1.0))
        st = state[hd]
        cross = jnp.dot(qb, st.astype(BF16), preferred_element_type=F32) * xi
        zeta = jnp.exp(lg * (float(L - 1) - idx))
        kz = (k * zeta).astype(BF16)
        state[hd] = st * float(np.exp(np.float32(lg) * np.float32(L))) + lax.dot_general(
            kz, vb, (((0,), (0,)), ((), ())), preferred_element_type=F32)
        y = intra + cross
        mu = jnp.mean(y, axis=-1, keepdims=True)
        cen = y - mu
        var = jnp.mean(cen * cen, axis=-1, keepdims=True)
        yn = cen * lax.rsqrt(var + EPS) * gng_ref[hd:hd + 1, :] + gnb_ref[hd:hd + 1, :]
        gate = proj[:, g_off + hd * dv:g_off + (hd + 1) * dv]
        ybuf[:, hd * dv:(hd + 1) * dv] = (_silu(gate) * yn).astype(BF16)

    out = jnp.dot(ybuf[...], wout_ref[...], preferred_element_type=F32)
    o_ref[...] = x + mod_ref[2:3, :] * out


def _ret_layer(x, mod, g, cos, sin, w_in, gn_g, gn_b, w_out, ret_layer):
    b, s, d = x.shape
    L = RET_L
    heads = RET_HEADS
    dk, dv = d // heads, 2 * d // heads
    n_in = w_in.shape[-1]
    return pl.pallas_call(
        _ret_kernel,
        out_shape=jax.ShapeDtypeStruct(x.shape, F32),
        grid=(b, s // L),
        in_specs=[
            pl.BlockSpec((None, L, d), lambda bi, j: (bi, j, 0)),
            pl.BlockSpec((None, 6, d), lambda bi, j: (bi, 0, 0)),
            _resident((1, d)),
            pl.BlockSpec((L, dk // 2), lambda bi, j: (j, 0)),
            pl.BlockSpec((L, dk // 2), lambda bi, j: (j, 0)),
            _resident((d, n_in), ret_layer),
            _resident((heads, dv)),
            _resident((heads, dv)),
            _resident((heads * dv, d), ret_layer),
        ],
        out_specs=pl.BlockSpec((None, L, d), lambda bi, j: (bi, j, 0)),
        scratch_shapes=[
            pltpu.VMEM((L, n_in), F32),
            pltpu.VMEM((L, heads * dv), BF16),
            pltpu.VMEM((heads, dk, dv), F32),
            pltpu.VMEM((heads, L, L), F32),
        ],
        compiler_params=pltpu.CompilerParams(
            dimension_semantics=("arbitrary", "arbitrary"),
            vmem_limit_bytes=V7X_VMEM_LIMIT_BYTES),
        name="retention_mixer",
    )(x, mod, g.reshape(1, d), cos, sin, w_in, gn_g, gn_b, w_out)


def _rope_tables(seq, dk):
    pos = jnp.arange(seq, dtype=F32)
    inv = ROPE_BASE ** (-jnp.arange(0, dk, 2, dtype=F32) / dk)
    ang = pos[:, None] * inv[None, :]
    return jnp.cos(ang), jnp.sin(ang)


def kernel(x, c, ada_w, ada_b, norm_mix_g, norm_mlp_g, conv_w_pw1, conv_b_pw1, conv_w_dw, conv_b_dw, conv_ln_g, conv_ln_b, conv_w_pw2, conv_b_pw2, ret_w_in, ret_gn_g, ret_gn_b, ret_w_out, mlp_w1, mlp_w2, final_norm_g):
    depth = ada_w.shape[0]
    b, s, d = x.shape
    assert s % CONV_TM == 0 and s % MLP_TM == 0 and s % RET_L == 0 and RET_L % CHUNK == 0
    assert CONV_HALO >= CONV_WIDTH - 1 and CONV_HALO % V7X_SUBLANES == 0 and CONV_TM % CONV_ROWS == 0
    assert d == 2 * MLP_PHASES * V7X_LANES

    mod = _ada(c, ada_w, ada_b).reshape(depth, b, 6, d)
    cos, sin = _rope_tables(s, d // RET_HEADS)
    pw1, pw2 = conv_w_pw1.astype(BF16), conv_w_pw2.astype(BF16)
    w_in, w_out = ret_w_in.astype(BF16), ret_w_out.astype(BF16)
    m1_blocks, m2_blocks = _mlp_weight_blocks(mlp_w1, mlp_w2)
    for i in range(depth):
        jm = i // 2
        if i % 2 == 0:
            assert i != depth - 1
            x = _conv_mlp_layer(x, mod[i], norm_mix_g[i], pw1, conv_b_pw1[jm], conv_w_dw[jm], conv_b_dw[jm],
                                conv_ln_g[jm], conv_ln_b[jm], pw2, conv_b_pw2[jm], norm_mlp_g[i],
                                m1_blocks, m2_blocks, conv_layer=jm, mlp_layer=i)
        else:
            x = _ret_layer(x, mod[i], norm_mix_g[i], cos, sin, w_in, ret_gn_g[jm], ret_gn_b[jm], w_out, ret_layer=jm)
            x = _mlp_layer(x, mod[i], norm_mlp_g[i], m1_blocks, m2_blocks, i, final_norm_g,
                           final=(i == depth - 1))
    return x
```

```python
import functools

import numpy as np
import jax
import jax.numpy as jnp
from jax import lax
from jax.experimental import pallas as pl
from jax.experimental.pallas import tpu as pltpu

F32 = jnp.float32
BF16 = jnp.bfloat16

EPS = 1e-6
CHUNK = 64
CONV_WIDTH = 31
RET_HEADS = 4
ROPE_BASE = 10000.0

V7X_SUBLANES = 8
V7X_LANES = 128
V7X_MXU_COLS = 256
V7X_VMEM_LIMIT_BYTES = 56 * 1024 * 1024

CONV_TM = 512
CONV_HALO = 32
CONV_ROWS = 32
MLP_PHASES = 4
MLP_TM = 512
RET_L = 256


def _resident(shape, layer=None):
    zeros = (0,) * len(shape)
    if layer is None:
        return pl.BlockSpec(shape, lambda *_: zeros, pipeline_mode=pl.Buffered(1))
    return pl.BlockSpec((None,) + tuple(shape), lambda *_: (layer,) + zeros, pipeline_mode=pl.Buffered(1))


def _mlp_weight_blocks(w1, w2):
    layers, d, d_ff = w1.shape
    w1b = w1.astype(BF16).reshape(layers, d, MLP_PHASES, d_ff // MLP_PHASES).transpose(0, 2, 1, 3)
    w2b = w2.astype(BF16).reshape(layers, d_ff, MLP_PHASES, d // MLP_PHASES).transpose(0, 2, 1, 3)
    return w1b, w2b


def _rmsnorm_mod(x, g, shift, scale):
    y = x * lax.rsqrt(jnp.mean(x * x, axis=-1, keepdims=True) + EPS)
    return (y * g) * (1.0 + scale) + shift


def _silu(v):
    return v * jax.nn.sigmoid(v)


def _after(value, dep):
    bits = pltpu.bitcast(dep, jnp.uint32)
    zero = lax.shift_right_logical(lax.shift_right_logical(bits, jnp.uint32(16)), jnp.uint32(16))
    return pltpu.bitcast(pltpu.bitcast(value, jnp.uint32) + zero, F32)


def _ada_kernel(c_ref, w_ref, b_ref, o_ref):
    cond = _silu(c_ref[...])
    o_ref[...] = jnp.dot(cond.astype(BF16), w_ref[...].astype(BF16),
                         preferred_element_type=F32) + b_ref[...]


def _ada(c, ada_w, ada_b):
    depth, d, n = ada_w.shape
    b = c.shape[0]
    tn = n // 4
    return pl.pallas_call(
        _ada_kernel,
        out_shape=jax.ShapeDtypeStruct((depth, b, n), F32),
        grid=(depth, n // tn),
        in_specs=[
            pl.BlockSpec((b, d), lambda l, j: (0, 0)),
            pl.BlockSpec((None, d, tn), lambda l, j: (l, 0, j)),
            pl.BlockSpec((None, 1, tn), lambda l, j: (l, 0, j)),
        ],
        out_specs=pl.BlockSpec((None, b, tn), lambda l, j: (l, 0, j)),
        compiler_params=pltpu.CompilerParams(
            dimension_semantics=("arbitrary", "arbitrary"),
            vmem_limit_bytes=V7X_VMEM_LIMIT_BYTES),
        name="ada_mod",
    )(c, ada_w, ada_b.reshape(depth, 1, n))


def _conv_mlp_kernel(x_ref, modc_ref, modp_ref, gmix_ref, w1_ref, b1_ref, wdw_ref, bdw_ref,
                     lng_ref, lnb_ref, w2_ref, b2_ref, gmlp_ref, m1_ref, m2_ref, o_ref,
                     ubuf, cbuf, x1buf, h2buf, tbuf, fbuf, *, tiles_per_seq):
    tm, d = x_ref.shape
    groups = d // V7X_LANES
    fw = m1_ref.shape[2]
    n = pl.program_id(0)

    @pl.when(n == 0)
    def _():
        x1buf[...] = jnp.zeros(x1buf.shape, F32)

    @pl.when(n % tiles_per_seq == 0)
    def _():
        ubuf[:, 0:CONV_HALO, :] = jnp.zeros((groups, CONV_HALO, V7X_LANES), F32)

    @pl.when(n % tiles_per_seq != 0)
    def _():
        ubuf[:, 0:CONV_HALO, :] = ubuf[:, tm:tm + CONV_HALO, :]

    h = _rmsnorm_mod(x_ref[...], gmix_ref[...], modc_ref[0:1, :], modc_ref[1:2, :]).astype(BF16)
    for c0 in range(0, d, V7X_MXU_COLS):
        a = jnp.dot(h, w1_ref[:, c0:c0 + V7X_MXU_COLS], preferred_element_type=F32) + b1_ref[:, c0:c0 + V7X_MXU_COLS]
        gt = (jnp.dot(h, w1_ref[:, d + c0:d + c0 + V7X_MXU_COLS], preferred_element_type=F32)
              + b1_ref[:, d + c0:d + c0 + V7X_MXU_COLS])
        u = a * jax.nn.sigmoid(gt)
        for k in range(V7X_MXU_COLS // V7X_LANES):
            ubuf[c0 // V7X_LANES + k, CONV_HALO:CONV_HALO + tm, :] = u[:, k * V7X_LANES:(k + 1) * V7X_LANES]
    h2buf[...] = _rmsnorm_mod(x1buf[...], gmlp_ref[...], modp_ref[3:4, :], modp_ref[4:5, :]).astype(BF16)

    first_tap = CONV_HALO - (CONV_WIDTH - 1)

    def conv_taps(gi):
        done = None
        for r0 in range(0, tm, CONV_ROWS):
            win = ubuf.at[gi, r0:r0 + CONV_ROWS + CONV_HALO, :]
            tap = jnp.zeros((CONV_ROWS, V7X_LANES), F32) + bdw_ref[gi]
            if done is not None:
                tap = _after(tap, done)
            for t in range(CONV_WIDTH):
                tap = tap + win[pl.ds(first_tap + t, CONV_ROWS, stride=1), :] * wdw_ref[gi, t:t + 1, :]
            cbuf[gi, r0:r0 + CONV_ROWS, :] = tap
            done = tap

    def hidden_phase(p, carry):
        for c0 in range(0, fw, V7X_MXU_COLS):
            t1 = jnp.dot(h2buf[...], m1_ref[p, :, c0:c0 + V7X_MXU_COLS], preferred_element_type=F32)
            t1 = jnp.maximum(t1, 0.0)
            tbuf[p, :, c0:c0 + V7X_MXU_COLS] = (t1 * t1).astype(BF16)
        conv_taps(p)
        return carry

    lax.fori_loop(0, MLP_PHASES, hidden_phase, 0)

    def out_phase(p, carry):
        ff = jnp.dot(tbuf[0], m2_ref[p, 0:fw, :], preferred_element_type=F32)
        for q in range(1, MLP_PHASES):
            ff = ff + jnp.dot(tbuf[q], m2_ref[p, q * fw:(q + 1) * fw, :], preferred_element_type=F32)
        conv_taps(MLP_PHASES + p)
        fbuf[p] = ff
        return carry

    lax.fori_loop(0, MLP_PHASES, out_phase, 0)

    ff = jnp.concatenate([fbuf[p] for p in range(MLP_PHASES)], axis=-1)
    o_ref[...] = x1buf[...] + modp_ref[5:6, :] * ff

    cv = jnp.concatenate([cbuf[gi] for gi in range(groups)], axis=-1)
    mu = jnp.mean(cv, axis=-1, keepdims=True)
    cen = cv - mu
    var = jnp.mean(cen * cen, axis=-1, keepdims=True)
    v = cen * lax.rsqrt(var + EPS) * lng_ref[...] + lnb_ref[...]
    vb = _silu(v).astype(BF16)
    for c0 in range(0, d, V7X_MXU_COLS):
        cols = slice(c0, c0 + V7X_MXU_COLS)
        y = jnp.dot(vb, w2_ref[:, cols], preferred_element_type=F32) + b2_ref[:, cols]
        x1buf[:, cols] = x_ref[:, cols] + modc_ref[2:3, cols] * y


def _conv_mlp_layer(x, mod, gmix, w1, b1, wdw, bdw, lng, lnb, w2, b2, gmlp, m1_blocks, m2_blocks,
                    conv_layer, mlp_layer):
    b, s, d = x.shape
    fw = m1_blocks.shape[-1]
    ow = m2_blocks.shape[-1]
    d_ff = fw * MLP_PHASES
    tm = CONV_TM
    tiles_per_seq = s // tm
    n_tiles = b * tiles_per_seq
    groups = d // V7X_LANES
    row = lambda v: v.reshape(1, -1)
    cur = lambda n: jnp.minimum(n, n_tiles - 1)
    prev = lambda n: jnp.maximum(n - 1, 0)
    out = pl.pallas_call(
        functools.partial(_conv_mlp_kernel, tiles_per_seq=tiles_per_seq),
        out_shape=jax.ShapeDtypeStruct((b * s, d), F32),
        grid=(n_tiles + 1,),
        in_specs=[
            pl.BlockSpec((tm, d), lambda n: (cur(n), 0)),
            pl.BlockSpec((None, 6, d), lambda n: (cur(n) // tiles_per_seq, 0, 0)),
            pl.BlockSpec((None, 6, d), lambda n: (prev(n) // tiles_per_seq, 0, 0)),
            _resident((1, d)),
            _resident((d, 2 * d), conv_layer),
            _resident((1, 2 * d)),
            _resident((groups, CONV_WIDTH, V7X_LANES)),
            _resident((groups, 1, V7X_LANES)),
            _resident((1, d)),
            _resident((1, d)),
            _resident((d, d), conv_layer),
            _resident((1, d)),
            _resident((1, d)),
            _resident((MLP_PHASES, d, fw), mlp_layer),
            _resident((MLP_PHASES, d_ff, ow), mlp_layer),
        ],
        out_specs=pl.BlockSpec((tm, d), lambda n: (prev(n), 0)),
        scratch_shapes=[
            pltpu.VMEM((groups, CONV_HALO + tm, V7X_LANES), F32),
            pltpu.VMEM((groups, tm, V7X_LANES), F32),
            pltpu.VMEM((tm, d), F32),
            pltpu.VMEM((tm, d), BF16),
            pltpu.VMEM((MLP_PHASES, tm, fw), BF16),
            pltpu.VMEM((MLP_PHASES, tm, ow), F32),
        ],
        compiler_params=pltpu.CompilerParams(
            dimension_semantics=("arbitrary",),
            vmem_limit_bytes=V7X_VMEM_LIMIT_BYTES),
        name="conv_mlp",
    )(x.reshape(b * s, d), mod, mod, row(gmix), w1, row(b1),
      wdw.reshape(CONV_WIDTH, groups, V7X_LANES).transpose(1, 0, 2), bdw.reshape(groups, 1, V7X_LANES),
      row(lng), row(lnb), w2, row(b2), row(gmlp), m1_blocks, m2_blocks)
    return out.reshape(b, s, d)


def _mlp_kernel(x_ref, mod_ref, g_ref, w1_ref, w2_ref, fg_ref, o_ref, *, final):
    fw = w1_ref.shape[2]
    x = x_ref[...]
    h = _rmsnorm_mod(x, g_ref[...], mod_ref[3:4, :], mod_ref[4:5, :]).astype(BF16)
    hidden = []
    for q in range(MLP_PHASES):
        t = jnp.maximum(jnp.dot(h, w1_ref[q], preferred_element_type=F32), 0.0)
        hidden.append((t * t).astype(BF16))
    blocks = []
    for p in range(MLP_PHASES):
        ff = jnp.dot(hidden[0], w2_ref[p, 0:fw, :], preferred_element_type=F32)
        for q in range(1, MLP_PHASES):
            ff = ff + jnp.dot(hidden[q], w2_ref[p, q * fw:(q + 1) * fw, :], preferred_element_type=F32)
        blocks.append(ff)
    y = x + mod_ref[5:6, :] * jnp.concatenate(blocks, axis=-1)
    if final:
        y = y * lax.rsqrt(jnp.mean(y * y, axis=-1, keepdims=True) + EPS) * fg_ref[...]
    o_ref[...] = y


def _mlp_layer(x, mod, g, w1_blocks, w2_blocks, layer, fg, final):
    b, s, d = x.shape
    fw, ow = w1_blocks.shape[-1], w2_blocks.shape[-1]
    d_ff = fw * MLP_PHASES
    tm = MLP_TM
    return pl.pallas_call(
        functools.partial(_mlp_kernel, final=final),
        out_shape=jax.ShapeDtypeStruct(x.shape, F32),
        grid=(b, s // tm),
        in_specs=[
            pl.BlockSpec((None, tm, d), lambda bi, j: (bi, j, 0)),
            pl.BlockSpec((None, 6, d), lambda bi, j: (bi, 0, 0)),
            _resident((1, d)),
            _resident((MLP_PHASES, d, fw), layer),
            _resident((MLP_PHASES, d_ff, ow), layer),
            _resident((1, d)),
        ],
        out_specs=pl.BlockSpec((None, tm, d), lambda bi, j: (bi, j, 0)),
        compiler_params=pltpu.CompilerParams(
            dimension_semantics=("arbitrary", "arbitrary"),
            vmem_limit_bytes=V7X_VMEM_LIMIT_BYTES),
        name="mlp_final" if final else "mlp",
    )(x, mod, g.reshape(1, d), w1_blocks, w2_blocks, fg.reshape(1, d))


def _log_gamma(head):
    return float(np.log(np.float32(1.0) - np.float32(2.0) ** np.float32(-5.0 - head)))


def _ret_kernel(x_ref, mod_ref, g_ref, cos_ref, sin_ref, win_ref, gng_ref, gnb_ref, wout_ref,
                o_ref, proj, ybuf, state, dmask):
    L, d = x_ref.shape
    heads = RET_HEADS
    dk = d // heads
    dv = 2 * d // heads
    half = dk // 2
    j = pl.program_id(1)

    @pl.when(j == 0)
    def _():
        state[...] = jnp.zeros(state.shape, F32)
        n = lax.broadcasted_iota(jnp.int32, (L, L), 0)
        m = lax.broadcasted_iota(jnp.int32, (L, L), 1)
        dist = jnp.abs(n - m).astype(F32)
        chunk_shift = CHUNK.bit_length() - 1
        visible = jnp.right_shift(m, chunk_shift) <= jnp.right_shift(n, chunk_shift)
        for hd in range(heads):
            dmask[hd] = jnp.where(visible, jnp.exp(_log_gamma(hd) * dist), 0.0)

    x = x_ref[...]
    h = _rmsnorm_mod(x, g_ref[...], mod_ref[0:1, :], mod_ref[1:2, :]).astype(BF16)
    n_in = win_ref.shape[1]
    for c0 in range(0, n_in, d):
        proj[:, c0:c0 + d] = jnp.dot(h, win_ref[:, c0:c0 + d], preferred_element_type=F32)

    cos = cos_ref[...]
    sin = sin_ref[...]
    idx = lax.broadcasted_iota(jnp.int32, (L, 1), 0).astype(F32)
    k_off, v_off, g_off = d, 2 * d, 2 * d + heads * dv

    def rope(base):
        x1 = proj[:, base:base + half]
        x2 = proj[:, base + half:base + dk]
        return jnp.concatenate([x1 * cos - x2 * sin, x2 * cos + x1 * sin], axis=-1)

    for hd in range(heads):
        lg = _log_gamma(hd)
        q = rope(hd * dk)
        k = rope(k_off + hd * dk) * (dk ** -0.5)
        vb = proj[:, v_off + hd * dv:v_off + (hd + 1) * dv].astype(BF16)
        scores = lax.dot_general(q.astype(BF16), k.astype(BF16), (((1,), (1,)), ((), ())),
                                 preferred_element_type=F32) * dmask[hd]
        xi = jnp.exp(lg * (idx + 1.0))
        st = state[hd]
        y = (jnp.dot(scores.astype(BF16), vb, preferred_element_type=F32)
             + jnp.dot((q * xi).astype(BF16), st.astype(BF16), preferred_element_type=F32))
        zeta = jnp.exp(lg * (float(L - 1) - idx))
        kz = (k * zeta).astype(BF16)
        state[hd] = st * float(np.exp(np.float32(lg) * np.float32(L))) + lax.dot_general(
            kz, vb, (((0,), (0,)), ((), ())), preferred_element_type=F32)
        mu = jnp.mean(y, axis=-1, keepdims=True)
        cen = y - mu
        var = jnp.mean(cen * cen, axis=-1, keepdims=True)
        yn = cen * lax.rsqrt(var + EPS) * gng_ref[hd:hd + 1, :] + gnb_ref[hd:hd + 1, :]
        gate = proj[:, g_off + hd * dv:g_off + (hd + 1) * dv]
        ybuf[:, hd * dv:(hd + 1) * dv] = (_silu(gate) * yn).astype(BF16)

    out = jnp.dot(ybuf[...], wout_ref[...], preferred_element_type=F32)
    o_ref[...] = x + mod_ref[2:3, :] * out


def _ret_layer(x, mod, g, cos, sin, w_in, gn_g, gn_b, w_out, ret_layer):
    b, s, d = x.shape
    L = RET_L
    heads = RET_HEADS
    dk, dv = d // heads, 2 * d // heads
    n_in = w_in.shape[-1]
    return pl.pallas_call(
        _ret_kernel,
        out_shape=jax.ShapeDtypeStruct(x.shape, F32),
        grid=(b, s // L),
        in_specs=[
            pl.BlockSpec((None, L, d), lambda bi, j: (bi, j, 0)),
            pl.BlockSpec((None, 6, d), lambda bi, j: (bi, 0, 0)),
            _resident((1, d)),
            pl.BlockSpec((L, dk // 2), lambda bi, j: (j, 0)),
            pl.BlockSpec((L, dk // 2), lambda bi, j: (j, 0)),
            _resident((d, n_in), ret_layer),
            _resident((heads, dv)),
            _resident((heads, dv)),
            _resident((heads * dv, d), ret_layer),
        ],
        out_specs=pl.BlockSpec((None, L, d), lambda bi, j: (bi, j, 0)),
        scratch_shapes=[
            pltpu.VMEM((L, n_in), F32),
            pltpu.VMEM((L, heads * dv), BF16),
            pltpu.VMEM((heads, dk, dv), F32),
            pltpu.VMEM((heads, L, L), F32),
        ],
        compiler_params=pltpu.CompilerParams(
            dimension_semantics=("arbitrary", "arbitrary"),
            vmem_limit_bytes=V7X_VMEM_LIMIT_BYTES),
        name="retention_mixer",
    )(x, mod, g.reshape(1, d), cos, sin, w_in, gn_g, gn_b, w_out)


def _rope_tables(seq, dk):
    pos = jnp.arange(seq, dtype=F32)
    inv = ROPE_BASE ** (-jnp.arange(0, dk, 2, dtype=F32) / dk)
    ang = pos[:, None] * inv[None, :]
    return jnp.cos(ang), jnp.sin(ang)


def kernel(x, c, ada_w, ada_b, norm_mix_g, norm_mlp_g, conv_w_pw1, conv_b_pw1, conv_w_dw, conv_b_dw, conv_ln_g, conv_ln_b, conv_w_pw2, conv_b_pw2, ret_w_in, ret_gn_g, ret_gn_b, ret_w_out, mlp_w1, mlp_w2, final_norm_g):
    depth = ada_w.shape[0]
    b, s, d = x.shape
    assert s % CONV_TM == 0 and s % MLP_TM == 0 and s % RET_L == 0 and RET_L % CHUNK == 0
    assert CONV_HALO >= CONV_WIDTH - 1 and CONV_HALO % V7X_SUBLANES == 0 and CONV_TM % CONV_ROWS == 0
    assert d == 2 * MLP_PHASES * V7X_LANES

    mod = _ada(c, ada_w, ada_b).reshape(depth, b, 6, d)
    cos, sin = _rope_tables(s, d // RET_HEADS)
    pw1, pw2 = conv_w_pw1.astype(BF16), conv_w_pw2.astype(BF16)
    w_in, w_out = ret_w_in.astype(BF16), ret_w_out.astype(BF16)
    m1_blocks, m2_blocks = _mlp_weight_blocks(mlp_w1, mlp_w2)
    for i in range(depth):
        jm = i // 2
        if i % 2 == 0:
            assert i != depth - 1
            x = _conv_mlp_layer(x, mod[i], norm_mix_g[i], pw1, conv_b_pw1[jm], conv_w_dw[jm], conv_b_dw[jm],
                                conv_ln_g[jm], conv_ln_b[jm], pw2, conv_b_pw2[jm], norm_mlp_g[i],
                                m1_blocks, m2_blocks, conv_layer=jm, mlp_layer=i)
        else:
            x = _ret_layer(x, mod[i], norm_mix_g[i], cos, sin, w_in, ret_gn_g[jm], ret_gn_b[jm], w_out, ret_layer=jm)
            x = _mlp_layer(x, mod[i], norm_mlp_g[i], m1_blocks, m2_blocks, i, final_norm_g,
                           final=(i == depth - 1))
    return x
```

```python
import functools

import numpy as np
import jax
import jax.numpy as jnp
from jax import lax
from jax.experimental import pallas as pl
from jax.experimental.pallas import tpu as pltpu

F32 = jnp.float32
BF16 = jnp.bfloat16

EPS = 1e-6
CHUNK = 64
CONV_WIDTH = 31
RET_HEADS = 4
ROPE_BASE = 10000.0

V7X_SUBLANES = 8
V7X_LANES = 128
V7X_MXU_COLS = 256
V7X_VMEM_LIMIT_BYTES = 56 * 1024 * 1024

CONV_TM = 512
CONV_HALO = 32
CONV_ROWS = 32
MLP_PHASES = 4
MLP_TM = 512
RET_L = 256


def _resident(shape, layer=None):
    zeros = (0,) * len(shape)
    if layer is None:
        return pl.BlockSpec(shape, lambda *_: zeros, pipeline_mode=pl.Buffered(1))
    return pl.BlockSpec((None,) + tuple(shape), lambda *_: (layer,) + zeros, pipeline_mode=pl.Buffered(1))


def _rmsnorm_mod(x, g, shift, scale):
    y = x * lax.rsqrt(jnp.mean(x * x, axis=-1, keepdims=True) + EPS)
    return (y * g) * (1.0 + scale) + shift


def _silu(v):
    return v * jax.nn.sigmoid(v)


def _after(value, dep):
    bits = pltpu.bitcast(dep, jnp.uint32)
    zero = lax.shift_right_logical(lax.shift_right_logical(bits, jnp.uint32(16)), jnp.uint32(16))
    return pltpu.bitcast(pltpu.bitcast(value, jnp.uint32) + zero, F32)


def _ada_kernel(c_ref, w_ref, b_ref, o_ref):
    cond = _silu(c_ref[...])
    o_ref[...] = jnp.dot(cond.astype(BF16), w_ref[...].astype(BF16),
                         preferred_element_type=F32) + b_ref[...]


def _ada(c, ada_w, ada_b):
    depth, d, n = ada_w.shape
    b = c.shape[0]
    tn = n // 4
    return pl.pallas_call(
        _ada_kernel,
        out_shape=jax.ShapeDtypeStruct((depth, b, n), F32),
        grid=(depth, n // tn),
        in_specs=[
            pl.BlockSpec((b, d), lambda l, j: (0, 0)),
            pl.BlockSpec((None, d, tn), lambda l, j: (l, 0, j)),
            pl.BlockSpec((None, 1, tn), lambda l, j: (l, 0, j)),
        ],
        out_specs=pl.BlockSpec((None, b, tn), lambda l, j: (l, 0, j)),
        compiler_params=pltpu.CompilerParams(
            dimension_semantics=("arbitrary", "arbitrary"),
            vmem_limit_bytes=V7X_VMEM_LIMIT_BYTES),
        name="ada_mod",
    )(c, ada_w, ada_b.reshape(depth, 1, n))


def _conv_mlp_kernel(x_ref, modc_ref, modp_ref, gmix_ref, w1_ref, b1_ref, wdw_ref, bdw_ref,
                     lng_ref, lnb_ref, w2_ref, b2_ref, gmlp_ref, m1_hbm, m2_hbm, o_ref,
                     ubuf, cbuf, x1buf, h2buf, tbuf, fbuf, m1_ref, m2_ref, wsem, *, tiles_per_seq, mlp_layer):
    tm, d = x_ref.shape
    groups = d // V7X_LANES
    fw = m1_ref.shape[2]
    ow = m2_ref.shape[2]
    n = pl.program_id(0)

    def weight_block_copies():
        copies = []
        for q in range(MLP_PHASES):
            copies.append(pltpu.make_async_copy(
                m1_hbm.at[mlp_layer, :, pl.ds(q * fw, fw)], m1_ref.at[q], wsem.at[q]))
            copies.append(pltpu.make_async_copy(
                m2_hbm.at[mlp_layer, :, pl.ds(q * ow, ow)], m2_ref.at[q], wsem.at[MLP_PHASES + q]))
        return copies

    @pl.when(n == 0)
    def _():
        for cp in weight_block_copies():
            cp.start()
        x1buf[...] = jnp.zeros(x1buf.shape, F32)
        for cp in weight_block_copies():
            cp.wait()

    @pl.when(n % tiles_per_seq == 0)
    def _():
        ubuf[:, 0:CONV_HALO, :] = jnp.zeros((groups, CONV_HALO, V7X_LANES), F32)

    @pl.when(n % tiles_per_seq != 0)
    def _():
        ubuf[:, 0:CONV_HALO, :] = ubuf[:, tm:tm + CONV_HALO, :]

    h = _rmsnorm_mod(x_ref[...], gmix_ref[...], modc_ref[0:1, :], modc_ref[1:2, :]).astype(BF16)
    for c0 in range(0, d, V7X_MXU_COLS):
        a = jnp.dot(h, w1_ref[:, c0:c0 + V7X_MXU_COLS], preferred_element_type=F32) + b1_ref[:, c0:c0 + V7X_MXU_COLS]
        gt = (jnp.dot(h, w1_ref[:, d + c0:d + c0 + V7X_MXU_COLS], preferred_element_type=F32)
              + b1_ref[:, d + c0:d + c0 + V7X_MXU_COLS])
        u = a * jax.nn.sigmoid(gt)
        for k in range(V7X_MXU_COLS // V7X_LANES):
            ubuf[c0 // V7X_LANES + k, CONV_HALO:CONV_HALO + tm, :] = u[:, k * V7X_LANES:(k + 1) * V7X_LANES]
    h2buf[...] = _rmsnorm_mod(x1buf[...], gmlp_ref[...], modp_ref[3:4, :], modp_ref[4:5, :]).astype(BF16)

    first_tap = CONV_HALO - (CONV_WIDTH - 1)

    def conv_taps(gi):
        done = None
        for r0 in range(0, tm, CONV_ROWS):
            win = ubuf.at[gi, r0:r0 + CONV_ROWS + CONV_HALO, :]
            tap = jnp.zeros((CONV_ROWS, V7X_LANES), F32) + bdw_ref[gi]
            if done is not None:
                tap = _after(tap, done)
            for t in range(CONV_WIDTH):
                tap = tap + win[pl.ds(first_tap + t, CONV_ROWS, stride=1), :] * wdw_ref[gi, t:t + 1, :]
            cbuf[gi, r0:r0 + CONV_ROWS, :] = tap
            done = tap

    def hidden_phase(p, carry):
        for c0 in range(0, fw, V7X_MXU_COLS):
            t1 = jnp.dot(h2buf[...], m1_ref[p, :, c0:c0 + V7X_MXU_COLS], preferred_element_type=F32)
            t1 = jnp.maximum(t1, 0.0)
            tbuf[p, :, c0:c0 + V7X_MXU_COLS] = (t1 * t1).astype(BF16)
        conv_taps(p)
        return carry

    lax.fori_loop(0, MLP_PHASES, hidden_phase, 0)

    def out_phase(p, carry):
        ff = jnp.dot(tbuf[0], m2_ref[p, 0:fw, :], preferred_element_type=F32)
        for q in range(1, MLP_PHASES):
            ff = ff + jnp.dot(tbuf[q], m2_ref[p, q * fw:(q + 1) * fw, :], preferred_element_type=F32)
        conv_taps(MLP_PHASES + p)
        fbuf[p] = ff
        return carry

    lax.fori_loop(0, MLP_PHASES, out_phase, 0)

    ff = jnp.concatenate([fbuf[p] for p in range(MLP_PHASES)], axis=-1)
    o_ref[...] = x1buf[...] + modp_ref[5:6, :] * ff

    cv = jnp.concatenate([cbuf[gi] for gi in range(groups)], axis=-1)
    mu = jnp.mean(cv, axis=-1, keepdims=True)
    cen = cv - mu
    var = jnp.mean(cen * cen, axis=-1, keepdims=True)
    v = cen * lax.rsqrt(var + EPS) * lng_ref[...] + lnb_ref[...]
    vb = _silu(v).astype(BF16)
    for c0 in range(0, d, V7X_MXU_COLS):
        cols = slice(c0, c0 + V7X_MXU_COLS)
        y = jnp.dot(vb, w2_ref[:, cols], preferred_element_type=F32) + b2_ref[:, cols]
        x1buf[:, cols] = x_ref[:, cols] + modc_ref[2:3, cols] * y


def _conv_mlp_layer(x, mod, gmix, w1, b1, wdw, bdw, lng, lnb, w2, b2, gmlp, m1, m2, conv_layer, mlp_layer):
    b, s, d = x.shape
    d_ff = m1.shape[-1]
    fw = d_ff // MLP_PHASES
    ow = d // MLP_PHASES
    tm = CONV_TM
    tiles_per_seq = s // tm
    n_tiles = b * tiles_per_seq
    groups = d // V7X_LANES
    row = lambda v: v.reshape(1, -1)
    cur = lambda n: jnp.minimum(n, n_tiles - 1)
    prev = lambda n: jnp.maximum(n - 1, 0)
    out = pl.pallas_call(
        functools.partial(_conv_mlp_kernel, tiles_per_seq=tiles_per_seq, mlp_layer=mlp_layer),
        out_shape=jax.ShapeDtypeStruct((b * s, d), F32),
        grid=(n_tiles + 1,),
        in_specs=[
            pl.BlockSpec((tm, d), lambda n: (cur(n), 0)),
            pl.BlockSpec((None, 6, d), lambda n: (cur(n) // tiles_per_seq, 0, 0)),
            pl.BlockSpec((None, 6, d), lambda n: (prev(n) // tiles_per_seq, 0, 0)),
            _resident((1, d)),
            _resident((d, 2 * d), conv_layer),
            _resident((1, 2 * d)),
            _resident((groups, CONV_WIDTH, V7X_LANES)),
            _resident((groups, 1, V7X_LANES)),
            _resident((1, d)),
            _resident((1, d)),
            _resident((d, d), conv_layer),
            _resident((1, d)),
            _resident((1, d)),
            pl.BlockSpec(memory_space=pl.ANY),
            pl.BlockSpec(memory_space=pl.ANY),
        ],
        out_specs=pl.BlockSpec((tm, d), lambda n: (prev(n), 0)),
        scratch_shapes=[
            pltpu.VMEM((groups, CONV_HALO + tm, V7X_LANES), F32),
            pltpu.VMEM((groups, tm, V7X_LANES), F32),
            pltpu.VMEM((tm, d), F32),
            pltpu.VMEM((tm, d), BF16),
            pltpu.VMEM((MLP_PHASES, tm, fw), BF16),
            pltpu.VMEM((MLP_PHASES, tm, ow), F32),
            pltpu.VMEM((MLP_PHASES, d, fw), BF16),
            pltpu.VMEM((MLP_PHASES, d_ff, ow), BF16),
            pltpu.SemaphoreType.DMA((2 * MLP_PHASES,)),
        ],
        compiler_params=pltpu.CompilerParams(
            dimension_semantics=("arbitrary",),
            vmem_limit_bytes=V7X_VMEM_LIMIT_BYTES),
        name="conv_mlp",
    )(x.reshape(b * s, d), mod, mod, row(gmix), w1, row(b1),
      wdw.reshape(CONV_WIDTH, groups, V7X_LANES).transpose(1, 0, 2), bdw.reshape(groups, 1, V7X_LANES),
      row(lng), row(lnb), w2, row(b2), row(gmlp), m1, m2)
    return out.reshape(b, s, d)


def _mlp_kernel(x_ref, mod_ref, g_ref, w1_ref, w2_ref, fg_ref, o_ref, *, final):
    d_ff = w1_ref.shape[1]
    fw = d_ff // MLP_PHASES
    x = x_ref[...]
    h = _rmsnorm_mod(x, g_ref[...], mod_ref[3:4, :], mod_ref[4:5, :]).astype(BF16)
    acc = jnp.zeros(x.shape, F32)
    for c0 in range(0, d_ff, fw):
        t = jnp.maximum(jnp.dot(h, w1_ref[:, c0:c0 + fw], preferred_element_type=F32), 0.0)
        acc = acc + jnp.dot((t * t).astype(BF16), w2_ref[c0:c0 + fw, :], preferred_element_type=F32)
    y = x + mod_ref[5:6, :] * acc
    if final:
        y = y * lax.rsqrt(jnp.mean(y * y, axis=-1, keepdims=True) + EPS) * fg_ref[...]
    o_ref[...] = y


def _mlp_layer(x, mod, g, w1, w2, layer, fg, final):
    b, s, d = x.shape
    d_ff = w1.shape[-1]
    tm = MLP_TM
    return pl.pallas_call(
        functools.partial(_mlp_kernel, final=final),
        out_shape=jax.ShapeDtypeStruct(x.shape, F32),
        grid=(b, s // tm),
        in_specs=[
            pl.BlockSpec((None, tm, d), lambda bi, j: (bi, j, 0)),
            pl.BlockSpec((None, 6, d), lambda bi, j: (bi, 0, 0)),
            _resident((1, d)),
            _resident((d, d_ff), layer),
            _resident((d_ff, d), layer),
            _resident((1, d)),
        ],
        out_specs=pl.BlockSpec((None, tm, d), lambda bi, j: (bi, j, 0)),
        compiler_params=pltpu.CompilerParams(
            dimension_semantics=("arbitrary", "arbitrary"),
            vmem_limit_bytes=V7X_VMEM_LIMIT_BYTES),
        name="mlp_final" if final else "mlp",
    )(x, mod, g.reshape(1, d), w1, w2, fg.reshape(1, d))


def _log_gamma(head):
    return float(np.log(np.float32(1.0) - np.float32(2.0) ** np.float32(-5.0 - head)))


def _ret_kernel(x_ref, mod_ref, g_ref, cos_ref, sin_ref, win_ref, gng_ref, gnb_ref, wout_ref,
                o_ref, proj, ybuf, state, dmask):
    L, d = x_ref.shape
    heads = RET_HEADS
    dk = d // heads
    dv = 2 * d // heads
    half = dk // 2
    j = pl.program_id(1)

    @pl.when(j == 0)
    def _():
        state[...] = jnp.zeros(state.shape, F32)
        n = lax.broadcasted_iota(jnp.int32, (L, L), 0)
        m = lax.broadcasted_iota(jnp.int32, (L, L), 1)
        dist = jnp.abs(n - m).astype(F32)
        chunk_shift = CHUNK.bit_length() - 1
        visible = jnp.right_shift(m, chunk_shift) <= jnp.right_shift(n, chunk_shift)
        for hd in range(heads):
            dmask[hd] = jnp.where(visible, jnp.exp(_log_gamma(hd) * dist), 0.0)

    x = x_ref[...]
    h = _rmsnorm_mod(x, g_ref[...], mod_ref[0:1, :], mod_ref[1:2, :]).astype(BF16)
    n_in = win_ref.shape[1]
    for c0 in range(0, n_in, d):
        proj[:, c0:c0 + d] = jnp.dot(h, win_ref[:, c0:c0 + d], preferred_element_type=F32)

    cos = cos_ref[...]
    sin = sin_ref[...]
    idx = lax.broadcasted_iota(jnp.int32, (L, 1), 0).astype(F32)
    k_off, v_off, g_off = d, 2 * d, 2 * d + heads * dv

    def rope(base):
        x1 = proj[:, base:base + half]
        x2 = proj[:, base + half:base + dk]
        return jnp.concatenate([x1 * cos - x2 * sin, x2 * cos + x1 * sin], axis=-1)

    for hd in range(heads):
        lg = _log_gamma(hd)
        q = rope(hd * dk)
        k = rope(k_off + hd * dk) * (dk ** -0.5)
        vb = proj[:, v_off + hd * dv:v_off + (hd + 1) * dv].astype(BF16)
        qb = q.astype(BF16)
        scores = lax.dot_general(qb, k.astype(BF16), (((1,), (1,)), ((), ())),
                                 preferred_element_type=F32) * dmask[hd]
        intra = jnp.dot(scores.astype(BF16), vb, preferred_element_type=F32)
        xi = jnp.exp(lg * (idx + 1.0))
        st = state[hd]
        cross = jnp.dot(qb, st.astype(BF16), preferred_element_type=F32) * xi
        y = intra + cross
        zeta = jnp.exp(lg * (float(L - 1) - idx))
        kz = (k * zeta).astype(BF16)
        state[hd] = st * float(np.exp(np.float32(lg) * np.float32(L))) + lax.dot_general(
            kz, vb, (((0,), (0,)), ((), ())), preferred_element_type=F32)
        mu = jnp.mean(y, axis=-1, keepdims=True)
        cen = y - mu
        var = jnp.mean(cen * cen, axis=-1, keepdims=True)
        yn = cen * lax.rsqrt(var + EPS) * gng_ref[hd:hd + 1, :] + gnb_ref[hd:hd + 1, :]
        gate = proj[:, g_off + hd * dv:g_off + (hd + 1) * dv]
        ybuf[:, hd * dv:(hd + 1) * dv] = (_silu(gate) * yn).astype(BF16)

    out = jnp.dot(ybuf[...], wout_ref[...], preferred_element_type=F32)
    o_ref[...] = x + mod_ref[2:3, :] * out


def _ret_layer(x, mod, g, cos, sin, w_in, gn_g, gn_b, w_out, ret_layer):
    b, s, d = x.shape
    L = RET_L
    heads = RET_HEADS
    dk, dv = d // heads, 2 * d // heads
    n_in = w_in.shape[-1]
    return pl.pallas_call(
        _ret_kernel,
        out_shape=jax.ShapeDtypeStruct(x.shape, F32),
        grid=(b, s // L),
        in_specs=[
            pl.BlockSpec((None, L, d), lambda bi, j: (bi, j, 0)),
            pl.BlockSpec((None, 6, d), lambda bi, j: (bi, 0, 0)),
            _resident((1, d)),
            pl.BlockSpec((L, dk // 2), lambda bi, j: (j, 0)),
            pl.BlockSpec((L, dk // 2), lambda bi, j: (j, 0)),
            _resident((d, n_in), ret_layer),
            _resident((heads, dv)),
            _resident((heads, dv)),
            _resident((heads * dv, d), ret_layer),
        ],
        out_specs=pl.BlockSpec((None, L, d), lambda bi, j: (bi, j, 0)),
        scratch_shapes=[
            pltpu.VMEM((L, n_in), F32),
            pltpu.VMEM((L, heads * dv), BF16),
            pltpu.VMEM((heads, dk, dv), F32),
            pltpu.VMEM((heads, L, L), F32),
        ],
        compiler_params=pltpu.CompilerParams(
            dimension_semantics=("arbitrary", "arbitrary"),
            vmem_limit_bytes=V7X_VMEM_LIMIT_BYTES),
        name="retention_mixer",
    )(x, mod, g.reshape(1, d), cos, sin, w_in, gn_g, gn_b, w_out)


def _rope_tables(seq, dk):
    pos = jnp.arange(seq, dtype=F32)
    inv = ROPE_BASE ** (-jnp.arange(0, dk, 2, dtype=F32) / dk)
    ang = pos[:, None] * inv[None, :]
    return jnp.cos(ang), jnp.sin(ang)


def kernel(x, c, ada_w, ada_b, norm_mix_g, norm_mlp_g, conv_w_pw1, conv_b_pw1, conv_w_dw, conv_b_dw, conv_ln_g, conv_ln_b, conv_w_pw2, conv_b_pw2, ret_w_in, ret_gn_g, ret_gn_b, ret_w_out, mlp_w1, mlp_w2, final_norm_g):
    depth = ada_w.shape[0]
    b, s, d = x.shape
    assert s % CONV_TM == 0 and s % MLP_TM == 0 and s % RET_L == 0 and RET_L % CHUNK == 0
    assert CONV_HALO >= CONV_WIDTH - 1 and CONV_HALO % V7X_SUBLANES == 0 and CONV_TM % CONV_ROWS == 0
    assert d == 2 * MLP_PHASES * V7X_LANES

    mod = _ada(c, ada_w, ada_b).reshape(depth, b, 6, d)
    cos, sin = _rope_tables(s, d // RET_HEADS)
    pw1, pw2 = conv_w_pw1.astype(BF16), conv_w_pw2.astype(BF16)
    w_in, w_out = ret_w_in.astype(BF16), ret_w_out.astype(BF16)
    m1, m2 = mlp_w1.astype(BF16), mlp_w2.astype(BF16)
    for i in range(depth):
        jm = i // 2
        if i % 2 == 0:
            assert i != depth - 1
            x = _conv_mlp_layer(x, mod[i], norm_mix_g[i], pw1, conv_b_pw1[jm], conv_w_dw[jm], conv_b_dw[jm],
                                conv_ln_g[jm], conv_ln_b[jm], pw2, conv_b_pw2[jm], norm_mlp_g[i],
                                m1, m2, conv_layer=jm, mlp_layer=i)
        else:
            x = _ret_layer(x, mod[i], norm_mix_g[i], cos, sin, w_in, ret_gn_g[jm], ret_gn_b[jm], w_out, ret_layer=jm)
            x = _mlp_layer(x, mod[i], norm_mlp_g[i], m1, m2, i, final_norm_g, final=(i == depth - 1))
    return x
```

```python
import functools

import numpy as np
import jax
import jax.numpy as jnp
from jax import lax
from jax.experimental import pallas as pl
from jax.experimental.pallas import tpu as pltpu

F32 = jnp.float32
BF16 = jnp.bfloat16

EPS = 1e-6
CHUNK = 64
CONV_WIDTH = 31
RET_HEADS = 4
ROPE_BASE = 10000.0

V7X_SUBLANES = 8
V7X_LANES = 128
V7X_MXU_COLS = 256
V7X_VMEM_LIMIT_BYTES = 56 * 1024 * 1024

CONV_TM = 512
CONV_HALO = 32
CONV_ROWS = 32
MLP_PHASES = 4
MLP_TM = 512
RET_L = 256


def _resident(shape, layer=None):
    zeros = (0,) * len(shape)
    if layer is None:
        return pl.BlockSpec(shape, lambda *_: zeros, pipeline_mode=pl.Buffered(1))
    return pl.BlockSpec((None,) + tuple(shape), lambda *_: (layer,) + zeros, pipeline_mode=pl.Buffered(1))


def _rmsnorm_mod(x, g, shift, scale):
    y = x * lax.rsqrt(jnp.mean(x * x, axis=-1, keepdims=True) + EPS)
    return (y * g) * (1.0 + scale) + shift


def _silu(v):
    return v * jax.nn.sigmoid(v)


def _after(value, dep):
    bits = pltpu.bitcast(dep, jnp.uint32)
    zero = lax.shift_right_logical(lax.shift_right_logical(bits, jnp.uint32(16)), jnp.uint32(16))
    return pltpu.bitcast(pltpu.bitcast(value, jnp.uint32) + zero, F32)


def _ada_kernel(c_ref, w_ref, b_ref, o_ref):
    cond = _silu(c_ref[...])
    o_ref[...] = jnp.dot(cond.astype(BF16), w_ref[...].astype(BF16),
                         preferred_element_type=F32) + b_ref[...]


def _ada(c, ada_w, ada_b):
    depth, d, n = ada_w.shape
    b = c.shape[0]
    tn = n // 4
    return pl.pallas_call(
        _ada_kernel,
        out_shape=jax.ShapeDtypeStruct((depth, b, n), F32),
        grid=(depth, n // tn),
        in_specs=[
            pl.BlockSpec((b, d), lambda l, j: (0, 0)),
            pl.BlockSpec((None, d, tn), lambda l, j: (l, 0, j)),
            pl.BlockSpec((None, 1, tn), lambda l, j: (l, 0, j)),
        ],
        out_specs=pl.BlockSpec((None, b, tn), lambda l, j: (l, 0, j)),
        compiler_params=pltpu.CompilerParams(
            dimension_semantics=("arbitrary", "arbitrary"),
            vmem_limit_bytes=V7X_VMEM_LIMIT_BYTES),
        name="ada_mod",
    )(c, ada_w, ada_b.reshape(depth, 1, n))


def _conv_mlp_kernel(x_ref, modc_ref, modp_ref, gmix_ref, w1_ref, b1_ref, wdw_ref, bdw_ref,
                     lng_ref, lnb_ref, w2_ref, b2_ref, gmlp_ref, m1_hbm, m2_hbm, o_ref,
                     ubuf, cbuf, x1buf, h2buf, tbuf, fbuf, m1_ref, m2_ref, wsem, *, tiles_per_seq, mlp_layer):
    tm, d = x_ref.shape
    groups = d // V7X_LANES
    fw = m1_ref.shape[2]
    ow = m2_ref.shape[2]
    n = pl.program_id(0)

    def weight_block_copies():
        copies = []
        for q in range(MLP_PHASES):
            copies.append(pltpu.make_async_copy(
                m1_hbm.at[mlp_layer, :, pl.ds(q * fw, fw)], m1_ref.at[q], wsem.at[q]))
            copies.append(pltpu.make_async_copy(
                m2_hbm.at[mlp_layer, :, pl.ds(q * ow, ow)], m2_ref.at[q], wsem.at[MLP_PHASES + q]))
        return copies

    @pl.when(n == 0)
    def _():
        for cp in weight_block_copies():
            cp.start()
        x1buf[...] = jnp.zeros(x1buf.shape, F32)
        for cp in weight_block_copies():
            cp.wait()

    @pl.when(n % tiles_per_seq == 0)
    def _():
        ubuf[:, 0:CONV_HALO, :] = jnp.zeros((groups, CONV_HALO, V7X_LANES), F32)

    @pl.when(n % tiles_per_seq != 0)
    def _():
        ubuf[:, 0:CONV_HALO, :] = ubuf[:, tm:tm + CONV_HALO, :]

    h = _rmsnorm_mod(x_ref[...], gmix_ref[...], modc_ref[0:1, :], modc_ref[1:2, :]).astype(BF16)
    for c0 in range(0, d, V7X_MXU_COLS):
        a = jnp.dot(h, w1_ref[:, c0:c0 + V7X_MXU_COLS], preferred_element_type=F32) + b1_ref[:, c0:c0 + V7X_MXU_COLS]
        gt = (jnp.dot(h, w1_ref[:, d + c0:d + c0 + V7X_MXU_COLS], preferred_element_type=F32)
              + b1_ref[:, d + c0:d + c0 + V7X_MXU_COLS])
        u = a * jax.nn.sigmoid(gt)
        for k in range(V7X_MXU_COLS // V7X_LANES):
            ubuf[c0 // V7X_LANES + k, CONV_HALO:CONV_HALO + tm, :] = u[:, k * V7X_LANES:(k + 1) * V7X_LANES]
    h2buf[...] = _rmsnorm_mod(x1buf[...], gmlp_ref[...], modp_ref[3:4, :], modp_ref[4:5, :]).astype(BF16)

    first_tap = CONV_HALO - (CONV_WIDTH - 1)

    def conv_taps(gi):
        done = None
        for r0 in range(0, tm, CONV_ROWS):
            win = ubuf.at[gi, r0:r0 + CONV_ROWS + CONV_HALO, :]
            tap = jnp.zeros((CONV_ROWS, V7X_LANES), F32) + bdw_ref[gi]
            if done is not None:
                tap = _after(tap, done)
            for t in range(CONV_WIDTH):
                tap = tap + win[pl.ds(first_tap + t, CONV_ROWS, stride=1), :] * wdw_ref[gi, t:t + 1, :]
            cbuf[gi, r0:r0 + CONV_ROWS, :] = tap
            done = tap

    def hidden_phase(p, carry):
        for c0 in range(0, fw, V7X_MXU_COLS):
            t1 = jnp.dot(h2buf[...], m1_ref[p, :, c0:c0 + V7X_MXU_COLS], preferred_element_type=F32)
            t1 = jnp.maximum(t1, 0.0)
            tbuf[p, :, c0:c0 + V7X_MXU_COLS] = (t1 * t1).astype(BF16)
        conv_taps(p)
        return carry

    lax.fori_loop(0, MLP_PHASES, hidden_phase, 0)

    def out_phase(p, carry):
        ff = jnp.dot(tbuf[0], m2_ref[p, 0:fw, :], preferred_element_type=F32)
        for q in range(1, MLP_PHASES):
            ff = ff + jnp.dot(tbuf[q], m2_ref[p, q * fw:(q + 1) * fw, :], preferred_element_type=F32)
        conv_taps(MLP_PHASES + p)
        fbuf[p] = ff
        return carry

    lax.fori_loop(0, MLP_PHASES, out_phase, 0)

    ff = jnp.concatenate([fbuf[p] for p in range(MLP_PHASES)], axis=-1)
    o_ref[...] = x1buf[...] + modp_ref[5:6, :] * ff

    cv = jnp.concatenate([cbuf[gi] for gi in range(groups)], axis=-1)
    mu = jnp.mean(cv, axis=-1, keepdims=True)
    cen = cv - mu
    var = jnp.mean(cen * cen, axis=-1, keepdims=True)
    v = cen * lax.rsqrt(var + EPS) * lng_ref[...] + lnb_ref[...]
    vb = _silu(v).astype(BF16)
    for c0 in range(0, d, V7X_MXU_COLS):
        cols = slice(c0, c0 + V7X_MXU_COLS)
        y = jnp.dot(vb, w2_ref[:, cols], preferred_element_type=F32) + b2_ref[:, cols]
        x1buf[:, cols] = x_ref[:, cols] + modc_ref[2:3, cols] * y


def _conv_mlp_layer(x, mod, gmix, w1, b1, wdw, bdw, lng, lnb, w2, b2, gmlp, m1, m2, conv_layer, mlp_layer):
    b, s, d = x.shape
    d_ff = m1.shape[-1]
    fw = d_ff // MLP_PHASES
    ow = d // MLP_PHASES
    tm = CONV_TM
    tiles_per_seq = s // tm
    n_tiles = b * tiles_per_seq
    groups = d // V7X_LANES
    row = lambda v: v.reshape(1, -1)
    cur = lambda n: jnp.minimum(n, n_tiles - 1)
    prev = lambda n: jnp.maximum(n - 1, 0)
    out = pl.pallas_call(
        functools.partial(_conv_mlp_kernel, tiles_per_seq=tiles_per_seq, mlp_layer=mlp_layer),
        out_shape=jax.ShapeDtypeStruct((b * s, d), F32),
        grid=(n_tiles + 1,),
        in_specs=[
            pl.BlockSpec((tm, d), lambda n: (cur(n), 0)),
            pl.BlockSpec((None, 6, d), lambda n: (cur(n) // tiles_per_seq, 0, 0)),
            pl.BlockSpec((None, 6, d), lambda n: (prev(n) // tiles_per_seq, 0, 0)),
            _resident((1, d)),
            _resident((d, 2 * d), conv_layer),
            _resident((1, 2 * d)),
            _resident((groups, CONV_WIDTH, V7X_LANES)),
            _resident((groups, 1, V7X_LANES)),
            _resident((1, d)),
            _resident((1, d)),
            _resident((d, d), conv_layer),
            _resident((1, d)),
            _resident((1, d)),
            pl.BlockSpec(memory_space=pl.ANY),
            pl.BlockSpec(memory_space=pl.ANY),
        ],
        out_specs=pl.BlockSpec((tm, d), lambda n: (prev(n), 0)),
        scratch_shapes=[
            pltpu.VMEM((groups, CONV_HALO + tm, V7X_LANES), F32),
            pltpu.VMEM((groups, tm, V7X_LANES), F32),
            pltpu.VMEM((tm, d), F32),
            pltpu.VMEM((tm, d), BF16),
            pltpu.VMEM((MLP_PHASES, tm, fw), BF16),
            pltpu.VMEM((MLP_PHASES, tm, ow), F32),
            pltpu.VMEM((MLP_PHASES, d, fw), BF16),
            pltpu.VMEM((MLP_PHASES, d_ff, ow), BF16),
            pltpu.SemaphoreType.DMA((2 * MLP_PHASES,)),
        ],
        compiler_params=pltpu.CompilerParams(
            dimension_semantics=("arbitrary",),
            vmem_limit_bytes=V7X_VMEM_LIMIT_BYTES),
        name="conv_mlp",
    )(x.reshape(b * s, d), mod, mod, row(gmix), w1, row(b1),
      wdw.reshape(CONV_WIDTH, groups, V7X_LANES).transpose(1, 0, 2), bdw.reshape(groups, 1, V7X_LANES),
      row(lng), row(lnb), w2, row(b2), row(gmlp), m1, m2)
    return out.reshape(b, s, d)


def _mlp_kernel(x_ref, mod_ref, g_ref, w1_ref, w2_ref, fg_ref, o_ref, *, final):
    d_ff = w1_ref.shape[1]
    fw = d_ff // MLP_PHASES
    x = x_ref[...]
    h = _rmsnorm_mod(x, g_ref[...], mod_ref[3:4, :], mod_ref[4:5, :]).astype(BF16)
    acc = jnp.zeros(x.shape, F32)
    for c0 in range(0, d_ff, fw):
        t = jnp.maximum(jnp.dot(h, w1_ref[:, c0:c0 + fw], preferred_element_type=F32), 0.0)
        acc = acc + jnp.dot((t * t).astype(BF16), w2_ref[c0:c0 + fw, :], preferred_element_type=F32)
    y = x + mod_ref[5:6, :] * acc
    if final:
        y = y * lax.rsqrt(jnp.mean(y * y, axis=-1, keepdims=True) + EPS) * fg_ref[...]
    o_ref[...] = y


def _mlp_layer(x, mod, g, w1, w2, layer, fg, final):
    b, s, d = x.shape
    d_ff = w1.shape[-1]
    tm = MLP_TM
    return pl.pallas_call(
        functools.partial(_mlp_kernel, final=final),
        out_shape=jax.ShapeDtypeStruct(x.shape, F32),
        grid=(b, s // tm),
        in_specs=[
            pl.BlockSpec((None, tm, d), lambda bi, j: (bi, j, 0)),
            pl.BlockSpec((None, 6, d), lambda bi, j: (bi, 0, 0)),
            _resident((1, d)),
            _resident((d, d_ff), layer),
            _resident((d_ff, d), layer),
            _resident((1, d)),
        ],
        out_specs=pl.BlockSpec((None, tm, d), lambda bi, j: (bi, j, 0)),
        compiler_params=pltpu.CompilerParams(
            dimension_semantics=("arbitrary", "arbitrary"),
            vmem_limit_bytes=V7X_VMEM_LIMIT_BYTES),
        name="mlp_final" if final else "mlp",
    )(x, mod, g.reshape(1, d), w1, w2, fg.reshape(1, d))


def _log_gamma(head):
    return float(np.log(np.float32(1.0) - np.float32(2.0) ** np.float32(-5.0 - head)))


def _ret_kernel(x_ref, xnext_ref, mod_ref, modnext_ref, g_ref, cos_ref, sin_ref, win_ref, gng_ref, gnb_ref,
                wout_ref, o_ref, hbuf, hnext, proj, ybuf, state, dmask, *, tiles_per_seq):
    L, d = x_ref.shape
    heads = RET_HEADS
    dk = d // heads
    dv = 2 * d // heads
    half = dk // 2
    n = pl.program_id(0)

    @pl.when(n == 0)
    def _():
        hnext[...] = _rmsnorm_mod(x_ref[...], g_ref[...], mod_ref[0:1, :], mod_ref[1:2, :]).astype(BF16)
        r = lax.broadcasted_iota(jnp.int32, (L, L), 0)
        c = lax.broadcasted_iota(jnp.int32, (L, L), 1)
        dist = jnp.abs(r - c).astype(F32)
        chunk_shift = CHUNK.bit_length() - 1
        visible = jnp.right_shift(c, chunk_shift) <= jnp.right_shift(r, chunk_shift)
        for hd in range(heads):
            dmask[hd] = jnp.where(visible, jnp.exp(_log_gamma(hd) * dist), 0.0)

    @pl.when(n % tiles_per_seq == 0)
    def _():
        state[...] = jnp.zeros(state.shape, F32)

    hbuf[...] = hnext[...]
    x = x_ref[...]
    h = hbuf[...]
    n_in = win_ref.shape[1]
    for c0 in range(0, n_in, d):
        proj[:, c0:c0 + d] = jnp.dot(h, win_ref[:, c0:c0 + d], preferred_element_type=F32)

    hn = _rmsnorm_mod(xnext_ref[...], g_ref[...], modnext_ref[0:1, :], modnext_ref[1:2, :]).astype(BF16)
    hnext[...] = hn

    cos = cos_ref[...]
    sin = sin_ref[...]
    last = hn[L - 2 * V7X_SUBLANES:L, d - V7X_LANES:d].astype(F32)[0:V7X_SUBLANES, :]
    cos = jnp.concatenate([_after(cos[0:V7X_SUBLANES, :], last), cos[V7X_SUBLANES:, :]], axis=0)
    idx = lax.broadcasted_iota(jnp.int32, (L, 1), 0).astype(F32)
    k_off, v_off, g_off = d, 2 * d, 2 * d + heads * dv

    def rope(base):
        x1 = proj[:, base:base + half]
        x2 = proj[:, base + half:base + dk]
        return jnp.concatenate([x1 * cos - x2 * sin, x2 * cos + x1 * sin], axis=-1)

    for hd in range(heads):
        lg = _log_gamma(hd)
        q = rope(hd * dk)
        k = rope(k_off + hd * dk) * (dk ** -0.5)
        vb = proj[:, v_off + hd * dv:v_off + (hd + 1) * dv].astype(BF16)
        qb = q.astype(BF16)
        scores = lax.dot_general(qb, k.astype(BF16), (((1,), (1,)), ((), ())),
                                 preferred_element_type=F32) * dmask[hd]
        intra = jnp.dot(scores.astype(BF16), vb, preferred_element_type=F32)
        xi = jnp.exp(lg * (idx + 1.0))
        st = state[hd]
        cross = jnp.dot(qb, st.astype(BF16), preferred_element_type=F32) * xi
        zeta = jnp.exp(lg * (float(L - 1) - idx))
        kz = (k * zeta).astype(BF16)
        state[hd] = st * float(np.exp(np.float32(lg) * np.float32(L))) + lax.dot_general(
            kz, vb, (((0,), (0,)), ((), ())), preferred_element_type=F32)
        y = intra + cross
        mu = jnp.mean(y, axis=-1, keepdims=True)
        cen = y - mu
        var = jnp.mean(cen * cen, axis=-1, keepdims=True)
        yn = cen * lax.rsqrt(var + EPS) * gng_ref[hd:hd + 1, :] + gnb_ref[hd:hd + 1, :]
        gate = proj[:, g_off + hd * dv:g_off + (hd + 1) * dv]
        ybuf[:, hd * dv:(hd + 1) * dv] = (_silu(gate) * yn).astype(BF16)

    out = jnp.dot(ybuf[...], wout_ref[...], preferred_element_type=F32)
    o_ref[...] = x + mod_ref[2:3, :] * out


def _ret_layer(x, mod, g, cos, sin, w_in, gn_g, gn_b, w_out, ret_layer):
    b, s, d = x.shape
    L = RET_L
    heads = RET_HEADS
    dk, dv = d // heads, 2 * d // heads
    n_in = w_in.shape[-1]
    tiles_per_seq = s // L
    n_tiles = b * tiles_per_seq
    nxt = lambda n: jnp.minimum(n + 1, n_tiles - 1)
    x2 = x.reshape(b * s, d)
    out = pl.pallas_call(
        functools.partial(_ret_kernel, tiles_per_seq=tiles_per_seq),
        out_shape=jax.ShapeDtypeStruct((b * s, d), F32),
        grid=(n_tiles,),
        in_specs=[
            pl.BlockSpec((L, d), lambda n: (n, 0)),
            pl.BlockSpec((L, d), lambda n: (nxt(n), 0)),
            pl.BlockSpec((None, 6, d), lambda n: (n // tiles_per_seq, 0, 0)),
            pl.BlockSpec((None, 6, d), lambda n: (nxt(n) // tiles_per_seq, 0, 0)),
            _resident((1, d)),
            pl.BlockSpec((L, dk // 2), lambda n: (n % tiles_per_seq, 0)),
            pl.BlockSpec((L, dk // 2), lambda n: (n % tiles_per_seq, 0)),
            _resident((d, n_in), ret_layer),
            _resident((heads, dv)),
            _resident((heads, dv)),
            _resident((heads * dv, d), ret_layer),
        ],
        out_specs=pl.BlockSpec((L, d), lambda n: (n, 0)),
        scratch_shapes=[
            pltpu.VMEM((L, d), BF16),
            pltpu.VMEM((L, d), BF16),
            pltpu.VMEM((L, n_in), F32),
            pltpu.VMEM((L, heads * dv), BF16),
            pltpu.VMEM((heads, dk, dv), F32),
            pltpu.VMEM((heads, L, L), F32),
        ],
        compiler_params=pltpu.CompilerParams(
            dimension_semantics=("arbitrary",),
            vmem_limit_bytes=V7X_VMEM_LIMIT_BYTES),
        name="retention_mixer",
    )(x2, x2, mod, mod, g.reshape(1, d), cos, sin, w_in, gn_g, gn_b, w_out)
    return out.reshape(b, s, d)


def _rope_tables(seq, dk):
    pos = jnp.arange(seq, dtype=F32)
    inv = ROPE_BASE ** (-jnp.arange(0, dk, 2, dtype=F32) / dk)
    ang = pos[:, None] * inv[None, :]
    return jnp.cos(ang), jnp.sin(ang)


def kernel(x, c, ada_w, ada_b, norm_mix_g, norm_mlp_g, conv_w_pw1, conv_b_pw1, conv_w_dw, conv_b_dw, conv_ln_g, conv_ln_b, conv_w_pw2, conv_b_pw2, ret_w_in, ret_gn_g, ret_gn_b, ret_w_out, mlp_w1, mlp_w2, final_norm_g):
    depth = ada_w.shape[0]
    b, s, d = x.shape
    assert s % CONV_TM == 0 and s % MLP_TM == 0 and s % RET_L == 0 and RET_L % CHUNK == 0
    assert CONV_HALO >= CONV_WIDTH - 1 and CONV_HALO % V7X_SUBLANES == 0 and CONV_TM % CONV_ROWS == 0
    assert d == 2 * MLP_PHASES * V7X_LANES

    mod = _ada(c, ada_w, ada_b).reshape(depth, b, 6, d)
    cos, sin = _rope_tables(s, d // RET_HEADS)
    pw1, pw2 = conv_w_pw1.astype(BF16), conv_w_pw2.astype(BF16)
    w_in, w_out = ret_w_in.astype(BF16), ret_w_out.astype(BF16)
    m1, m2 = mlp_w1.astype(BF16), mlp_w2.astype(BF16)
    for i in range(depth):
        jm = i // 2
        if i % 2 == 0:
            assert i != depth - 1
            x = _conv_mlp_layer(x, mod[i], norm_mix_g[i], pw1, conv_b_pw1[jm], conv_w_dw[jm], conv_b_dw[jm],
                                conv_ln_g[jm], conv_ln_b[jm], pw2, conv_b_pw2[jm], norm_mlp_g[i],
                                m1, m2, conv_layer=jm, mlp_layer=i)
        else:
            x = _ret_layer(x, mod[i], norm_mix_g[i], cos, sin, w_in, ret_gn_g[jm], ret_gn_b[jm], w_out, ret_layer=jm)
            x = _mlp_layer(x, mod[i], norm_mlp_g[i], m1, m2, i, final_norm_g, final=(i == depth - 1))
    return x
```

```python
import functools

import numpy as np
import jax
import jax.numpy as jnp
from jax import lax
from jax.experimental import pallas as pl
from jax.experimental.pallas import tpu as pltpu

F32 = jnp.float32
BF16 = jnp.bfloat16

EPS = 1e-6
CHUNK = 64
CONV_WIDTH = 31
RET_HEADS = 4
ROPE_BASE = 10000.0

V7X_SUBLANES = 8
V7X_LANES = 128
V7X_MXU_COLS = 256
V7X_VMEM_LIMIT_BYTES = 56 * 1024 * 1024

CONV_TM = 512
CONV_HALO = 32
CONV_ROWS = 32
MLP_PHASES = 4
MLP_TM = 1024
RET_L = 256
W_PIECE = (256, 1024)


def _resident(shape, layer=None):
    zeros = (0,) * len(shape)
    if layer is None:
        return pl.BlockSpec(shape, lambda *_: zeros, pipeline_mode=pl.Buffered(1))
    return pl.BlockSpec((None,) + tuple(shape), lambda *_: (layer,) + zeros, pipeline_mode=pl.Buffered(1))


def _fetch_weights(plan, stage, sems):
    copies = [pltpu.make_async_copy(src, stage.at[i % 2], sems.at[i % 2]) for i, (src, _) in enumerate(plan)]
    copies[0].start()
    for i, (_, store) in enumerate(plan):
        if i + 1 < len(plan):
            copies[i + 1].start()
        copies[i].wait()
        store(stage[i % 2].astype(BF16))


def _piece_plan(w_hbm, layer, store):
    _, rows, cols = w_hbm.shape
    pr, pc = W_PIECE
    assert rows % pr == 0 and cols % pc == 0
    return [(w_hbm.at[layer, pl.ds(r0, pr), pl.ds(c0, pc)], functools.partial(store, r0, c0))
            for c0 in range(0, cols, pc) for r0 in range(0, rows, pr)]


def _store_into(ref):
    def store(r0, c0, value):
        ref[r0:r0 + W_PIECE[0], c0:c0 + W_PIECE[1]] = value
    return store


def _rmsnorm_mod(x, g, shift, scale):
    y = x * lax.rsqrt(jnp.mean(x * x, axis=-1, keepdims=True) + EPS)
    return (y * g) * (1.0 + scale) + shift


def _silu(v):
    return v * jax.nn.sigmoid(v)


def _after(value, dep):
    bits = pltpu.bitcast(dep, jnp.uint32)
    zero = lax.shift_right_logical(lax.shift_right_logical(bits, jnp.uint32(16)), jnp.uint32(16))
    return pltpu.bitcast(pltpu.bitcast(value, jnp.uint32) + zero, F32)


def _ada_kernel(c_ref, w_ref, b_ref, o_ref):
    cond = _silu(c_ref[...])
    o_ref[...] = jnp.dot(cond.astype(BF16), w_ref[...].astype(BF16),
                         preferred_element_type=F32) + b_ref[...]


def _ada(c, ada_w, ada_b):
    depth, d, n = ada_w.shape
    b = c.shape[0]
    tn = n // 4
    return pl.pallas_call(
        _ada_kernel,
        out_shape=jax.ShapeDtypeStruct((depth, b, n), F32),
        grid=(depth, n // tn),
        in_specs=[
            pl.BlockSpec((b, d), lambda l, j: (0, 0)),
            pl.BlockSpec((None, d, tn), lambda l, j: (l, 0, j)),
            pl.BlockSpec((None, 1, tn), lambda l, j: (l, 0, j)),
        ],
        out_specs=pl.BlockSpec((None, b, tn), lambda l, j: (l, 0, j)),
        compiler_params=pltpu.CompilerParams(
            dimension_semantics=("arbitrary", "arbitrary"),
            vmem_limit_bytes=V7X_VMEM_LIMIT_BYTES),
        name="ada_mod",
    )(c, ada_w, ada_b.reshape(depth, 1, n))


def _conv_mlp_kernel(x_ref, modc_ref, modp_ref, gmix_ref, w1_hbm, b1_ref, wdw_ref, bdw_ref,
                     lng_ref, lnb_ref, w2_hbm, b2_ref, gmlp_ref, m1_hbm, m2_hbm, o_ref,
                     ubuf, cbuf, x1buf, h2buf, tbuf, fbuf, w1_ref, w2_ref, m1_ref, m2_ref, stage, wsem,
                     *, tiles_per_seq, conv_layer, mlp_layer):
    tm, d = x_ref.shape
    groups = d // V7X_LANES
    fw = m1_ref.shape[2]
    ow = m2_ref.shape[2]
    n = pl.program_id(0)

    @pl.when(n == 0)
    def _():
        x1buf[...] = jnp.zeros(x1buf.shape, F32)

        def store_m1(r0, c0, value):
            m1_ref[c0 // fw, r0:r0 + W_PIECE[0], c0 % fw:c0 % fw + W_PIECE[1]] = value

        def store_m2(r0, c0, value):
            for q in range(W_PIECE[1] // ow):
                m2_ref[c0 // ow + q, r0:r0 + W_PIECE[0], :] = value[:, q * ow:(q + 1) * ow]

        _fetch_weights(_piece_plan(w1_hbm, conv_layer, _store_into(w1_ref))
                       + _piece_plan(m1_hbm, mlp_layer, store_m1)
                       + _piece_plan(m2_hbm, mlp_layer, store_m2)
                       + _piece_plan(w2_hbm, conv_layer, _store_into(w2_ref)), stage, wsem)

    @pl.when(n % tiles_per_seq == 0)
    def _():
        ubuf[:, 0:CONV_HALO, :] = jnp.zeros((groups, CONV_HALO, V7X_LANES), F32)

    @pl.when(n % tiles_per_seq != 0)
    def _():
        ubuf[:, 0:CONV_HALO, :] = ubuf[:, tm:tm + CONV_HALO, :]

    h = _rmsnorm_mod(x_ref[...], gmix_ref[...], modc_ref[0:1, :], modc_ref[1:2, :]).astype(BF16)
    for c0 in range(0, d, V7X_MXU_COLS):
        a = jnp.dot(h, w1_ref[:, c0:c0 + V7X_MXU_COLS], preferred_element_type=F32) + b1_ref[:, c0:c0 + V7X_MXU_COLS]
        gt = (jnp.dot(h, w1_ref[:, d + c0:d + c0 + V7X_MXU_COLS], preferred_element_type=F32)
              + b1_ref[:, d + c0:d + c0 + V7X_MXU_COLS])
        u = a * jax.nn.sigmoid(gt)
        for k in range(V7X_MXU_COLS // V7X_LANES):
            ubuf[c0 // V7X_LANES + k, CONV_HALO:CONV_HALO + tm, :] = u[:, k * V7X_LANES:(k + 1) * V7X_LANES]
    h2buf[...] = _rmsnorm_mod(x1buf[...], gmlp_ref[...], modp_ref[3:4, :], modp_ref[4:5, :]).astype(BF16)

    first_tap = CONV_HALO - (CONV_WIDTH - 1)

    def conv_taps(gi):
        done = None
        for r0 in range(0, tm, CONV_ROWS):
            win = ubuf.at[gi, r0:r0 + CONV_ROWS + CONV_HALO, :]
            tap = jnp.zeros((CONV_ROWS, V7X_LANES), F32) + bdw_ref[gi]
            if done is not None:
                tap = _after(tap, done)
            for t in range(CONV_WIDTH):
                tap = tap + win[pl.ds(first_tap + t, CONV_ROWS, stride=1), :] * wdw_ref[gi, t:t + 1, :]
            cbuf[gi, r0:r0 + CONV_ROWS, :] = tap
            done = tap

    def hidden_phase(p, carry):
        for c0 in range(0, fw, V7X_MXU_COLS):
            t1 = jnp.dot(h2buf[...], m1_ref[p, :, c0:c0 + V7X_MXU_COLS], preferred_element_type=F32)
            t1 = jnp.maximum(t1, 0.0)
            tbuf[p, :, c0:c0 + V7X_MXU_COLS] = (t1 * t1).astype(BF16)
        conv_taps(p)
        return carry

    lax.fori_loop(0, MLP_PHASES, hidden_phase, 0)

    def out_phase(p, carry):
        ff = jnp.dot(tbuf[0], m2_ref[p, 0:fw, :], preferred_element_type=F32)
        for q in range(1, MLP_PHASES):
            ff = ff + jnp.dot(tbuf[q], m2_ref[p, q * fw:(q + 1) * fw, :], preferred_element_type=F32)
        conv_taps(MLP_PHASES + p)
        fbuf[p] = ff
        return carry

    lax.fori_loop(0, MLP_PHASES, out_phase, 0)

    ff = jnp.concatenate([fbuf[p] for p in range(MLP_PHASES)], axis=-1)
    o_ref[...] = x1buf[...] + modp_ref[5:6, :] * ff

    cv = jnp.concatenate([cbuf[gi] for gi in range(groups)], axis=-1)
    mu = jnp.mean(cv, axis=-1, keepdims=True)
    cen = cv - mu
    var = jnp.mean(cen * cen, axis=-1, keepdims=True)
    v = cen * lax.rsqrt(var + EPS) * lng_ref[...] + lnb_ref[...]
    vb = _silu(v).astype(BF16)
    for c0 in range(0, d, V7X_MXU_COLS):
        cols = slice(c0, c0 + V7X_MXU_COLS)
        y = jnp.dot(vb, w2_ref[:, cols], preferred_element_type=F32) + b2_ref[:, cols]
        x1buf[:, cols] = x_ref[:, cols] + modc_ref[2:3, cols] * y


def _conv_mlp_layer(x, mod, gmix, w1, b1, wdw, bdw, lng, lnb, w2, b2, gmlp, m1, m2, conv_layer, mlp_layer):
    b, s, d = x.shape
    d_ff = m1.shape[-1]
    fw = d_ff // MLP_PHASES
    ow = d // MLP_PHASES
    assert fw % W_PIECE[1] == 0 and W_PIECE[1] % ow == 0
    tm = CONV_TM
    tiles_per_seq = s // tm
    n_tiles = b * tiles_per_seq
    groups = d // V7X_LANES
    row = lambda v: v.reshape(1, -1)
    cur = lambda n: jnp.minimum(n, n_tiles - 1)
    prev = lambda n: jnp.maximum(n - 1, 0)
    out = pl.pallas_call(
        functools.partial(_conv_mlp_kernel, tiles_per_seq=tiles_per_seq, conv_layer=conv_layer,
                          mlp_layer=mlp_layer),
        out_shape=jax.ShapeDtypeStruct((b * s, d), F32),
        grid=(n_tiles + 1,),
        in_specs=[
            pl.BlockSpec((tm, d), lambda n: (cur(n), 0)),
            pl.BlockSpec((None, 6, d), lambda n: (cur(n) // tiles_per_seq, 0, 0)),
            pl.BlockSpec((None, 6, d), lambda n: (prev(n) // tiles_per_seq, 0, 0)),
            _resident((1, d)),
            pl.BlockSpec(memory_space=pl.ANY),
            _resident((1, 2 * d)),
            _resident((groups, CONV_WIDTH, V7X_LANES)),
            _resident((groups, 1, V7X_LANES)),
            _resident((1, d)),
            _resident((1, d)),
            pl.BlockSpec(memory_space=pl.ANY),
            _resident((1, d)),
            _resident((1, d)),
            pl.BlockSpec(memory_space=pl.ANY),
            pl.BlockSpec(memory_space=pl.ANY),
        ],
        out_specs=pl.BlockSpec((tm, d), lambda n: (prev(n), 0)),
        scratch_shapes=[
            pltpu.VMEM((groups, CONV_HALO + tm, V7X_LANES), F32),
            pltpu.VMEM((groups, tm, V7X_LANES), F32),
            pltpu.VMEM((tm, d), F32),
            pltpu.VMEM((tm, d), BF16),
            pltpu.VMEM((MLP_PHASES, tm, fw), BF16),
            pltpu.VMEM((MLP_PHASES, tm, ow), F32),
            pltpu.VMEM((d, 2 * d), BF16),
            pltpu.VMEM((d, d), BF16),
            pltpu.VMEM((MLP_PHASES, d, fw), BF16),
            pltpu.VMEM((MLP_PHASES, d_ff, ow), BF16),
            pltpu.VMEM((2,) + W_PIECE, F32),
            pltpu.SemaphoreType.DMA((2,)),
        ],
        compiler_params=pltpu.CompilerParams(
            dimension_semantics=("arbitrary",),
            vmem_limit_bytes=V7X_VMEM_LIMIT_BYTES),
        name="conv_mlp",
    )(x.reshape(b * s, d), mod, mod, row(gmix), w1, row(b1),
      wdw.reshape(CONV_WIDTH, groups, V7X_LANES).transpose(1, 0, 2), bdw.reshape(groups, 1, V7X_LANES),
      row(lng), row(lnb), w2, row(b2), row(gmlp), m1, m2)
    return out.reshape(b, s, d)


def _mlp_kernel(x_ref, mod_ref, g_ref, w1_hbm, w2_hbm, fg_ref, o_ref, w1_ref, w2_ref, stage, wsem,
                *, layer, final):
    d_ff = w1_ref.shape[1]
    fw = d_ff // MLP_PHASES

    @pl.when(jnp.logical_and(pl.program_id(0) == 0, pl.program_id(1) == 0))
    def _():
        _fetch_weights(_piece_plan(w1_hbm, layer, _store_into(w1_ref))
                       + _piece_plan(w2_hbm, layer, _store_into(w2_ref)), stage, wsem)

    x = x_ref[...]
    h = _rmsnorm_mod(x, g_ref[...], mod_ref[3:4, :], mod_ref[4:5, :]).astype(BF16)
    acc = jnp.zeros(x.shape, F32)
    for c0 in range(0, d_ff, fw):
        t = jnp.maximum(jnp.dot(h, w1_ref[:, c0:c0 + fw], preferred_element_type=F32), 0.0)
        acc = acc + jnp.dot((t * t).astype(BF16), w2_ref[c0:c0 + fw, :], preferred_element_type=F32)
    y = x + mod_ref[5:6, :] * acc
    if final:
        y = y * lax.rsqrt(jnp.mean(y * y, axis=-1, keepdims=True) + EPS) * fg_ref[...]
    o_ref[...] = y


def _mlp_layer(x, mod, g, w1, w2, layer, fg, final):
    b, s, d = x.shape
    d_ff = w1.shape[-1]
    tm = MLP_TM
    return pl.pallas_call(
        functools.partial(_mlp_kernel, layer=layer, final=final),
        out_shape=jax.ShapeDtypeStruct(x.shape, F32),
        grid=(b, s // tm),
        in_specs=[
            pl.BlockSpec((None, tm, d), lambda bi, j: (bi, j, 0)),
            pl.BlockSpec((None, 6, d), lambda bi, j: (bi, 0, 0)),
            _resident((1, d)),
            pl.BlockSpec(memory_space=pl.ANY),
            pl.BlockSpec(memory_space=pl.ANY),
            _resident((1, d)),
        ],
        out_specs=pl.BlockSpec((None, tm, d), lambda bi, j: (bi, j, 0)),
        scratch_shapes=[
            pltpu.VMEM((d, d_ff), BF16),
            pltpu.VMEM((d_ff, d), BF16),
            pltpu.VMEM((2,) + W_PIECE, F32),
            pltpu.SemaphoreType.DMA((2,)),
        ],
        compiler_params=pltpu.CompilerParams(
            dimension_semantics=("arbitrary", "arbitrary"),
            vmem_limit_bytes=V7X_VMEM_LIMIT_BYTES),
        name="mlp_final" if final else "mlp",
    )(x, mod, g.reshape(1, d), w1, w2, fg.reshape(1, d))


def _log_gamma(head):
    return float(np.log(np.float32(1.0) - np.float32(2.0) ** np.float32(-5.0 - head)))


def _ret_kernel(x_ref, xnext_ref, mod_ref, modnext_ref, g_ref, cos_ref, sin_ref, win_hbm, gng_ref, gnb_ref,
                wout_hbm, o_ref, hbuf, hnext, proj, ybuf, state, dmask, win_ref, wout_ref, stage, wsem,
                *, tiles_per_seq, ret_layer):
    L, d = x_ref.shape
    heads = RET_HEADS
    dk = d // heads
    dv = 2 * d // heads
    half = dk // 2
    n = pl.program_id(0)

    @pl.when(n == 0)
    def _():
        _fetch_weights(_piece_plan(win_hbm, ret_layer, _store_into(win_ref))
                       + _piece_plan(wout_hbm, ret_layer, _store_into(wout_ref)), stage, wsem)
        hnext[...] = _rmsnorm_mod(x_ref[...], g_ref[...], mod_ref[0:1, :], mod_ref[1:2, :]).astype(BF16)
        r = lax.broadcasted_iota(jnp.int32, (L, L), 0)
        c = lax.broadcasted_iota(jnp.int32, (L, L), 1)
        dist = jnp.abs(r - c).astype(F32)
        chunk_shift = CHUNK.bit_length() - 1
        visible = jnp.right_shift(c, chunk_shift) <= jnp.right_shift(r, chunk_shift)
        for hd in range(heads):
            dmask[hd] = jnp.where(visible, jnp.exp(_log_gamma(hd) * dist), 0.0)

    @pl.when(n % tiles_per_seq == 0)
    def _():
        state[...] = jnp.zeros(state.shape, F32)

    hbuf[...] = hnext[...]
    x = x_ref[...]
    h = hbuf[...]
    n_in = win_ref.shape[1]
    for c0 in range(0, n_in, d):
        proj[:, c0:c0 + d] = jnp.dot(h, win_ref[:, c0:c0 + d], preferred_element_type=F32)

    hn = _rmsnorm_mod(xnext_ref[...], g_ref[...], modnext_ref[0:1, :], modnext_ref[1:2, :]).astype(BF16)
    hnext[...] = hn

    cos = cos_ref[...]
    sin = sin_ref[...]
    last = hn[L - 2 * V7X_SUBLANES:L, d - V7X_LANES:d].astype(F32)[0:V7X_SUBLANES, :]
    cos = jnp.concatenate([_after(cos[0:V7X_SUBLANES, :], last), cos[V7X_SUBLANES:, :]], axis=0)
    idx = lax.broadcasted_iota(jnp.int32, (L, 1), 0).astype(F32)
    k_off, v_off, g_off = d, 2 * d, 2 * d + heads * dv

    def rope(base):
        x1 = proj[:, base:base + half]
        x2 = proj[:, base + half:base + dk]
        return jnp.concatenate([x1 * cos - x2 * sin, x2 * cos + x1 * sin], axis=-1)

    for hd in range(heads):
        lg = _log_gamma(hd)
        q = rope(hd * dk)
        k = rope(k_off + hd * dk) * (dk ** -0.5)
        vb = proj[:, v_off + hd * dv:v_off + (hd + 1) * dv].astype(BF16)
        qb = q.astype(BF16)
        scores = lax.dot_general(qb, k.astype(BF16), (((1,), (1,)), ((), ())),
                                 preferred_element_type=F32) * dmask[hd]
        intra = jnp.dot(scores.astype(BF16), vb, preferred_element_type=F32)
        xi = jnp.exp(lg * (idx + 1.0))
        st = state[hd]
        cross = jnp.dot(qb, st.astype(BF16), preferred_element_type=F32) * xi
        zeta = jnp.exp(lg * (float(L - 1) - idx))
        kz = (k * zeta).astype(BF16)
        state[hd] = st * float(np.exp(np.float32(lg) * np.float32(L))) + lax.dot_general(
            kz, vb, (((0,), (0,)), ((), ())), preferred_element_type=F32)
        y = intra + cross
        mu = jnp.mean(y, axis=-1, keepdims=True)
        cen = y - mu
        var = jnp.mean(cen * cen, axis=-1, keepdims=True)
        yn = cen * lax.rsqrt(var + EPS) * gng_ref[hd:hd + 1, :] + gnb_ref[hd:hd + 1, :]
        gate = proj[:, g_off + hd * dv:g_off + (hd + 1) * dv]
        ybuf[:, hd * dv:(hd + 1) * dv] = (_silu(gate) * yn).astype(BF16)

    out = jnp.dot(ybuf[...], wout_ref[...], preferred_element_type=F32)
    o_ref[...] = x + mod_ref[2:3, :] * out


def _ret_layer(x, mod, g, cos, sin, w_in, gn_g, gn_b, w_out, ret_layer):
    b, s, d = x.shape
    L = RET_L
    heads = RET_HEADS
    dk, dv = d // heads, 2 * d // heads
    n_in = w_in.shape[-1]
    tiles_per_seq = s // L
    n_tiles = b * tiles_per_seq
    nxt = lambda n: jnp.minimum(n + 1, n_tiles - 1)
    x2 = x.reshape(b * s, d)
    out = pl.pallas_call(
        functools.partial(_ret_kernel, tiles_per_seq=tiles_per_seq, ret_layer=ret_layer),
        out_shape=jax.ShapeDtypeStruct((b * s, d), F32),
        grid=(n_tiles,),
        in_specs=[
            pl.BlockSpec((L, d), lambda n: (n, 0)),
            pl.BlockSpec((L, d), lambda n: (nxt(n), 0)),
            pl.BlockSpec((None, 6, d), lambda n: (n // tiles_per_seq, 0, 0)),
            pl.BlockSpec((None, 6, d), lambda n: (nxt(n) // tiles_per_seq, 0, 0)),
            _resident((1, d)),
            pl.BlockSpec((L, dk // 2), lambda n: (n % tiles_per_seq, 0)),
            pl.BlockSpec((L, dk // 2), lambda n: (n % tiles_per_seq, 0)),
            pl.BlockSpec(memory_space=pl.ANY),
            _resident((heads, dv)),
            _resident((heads, dv)),
            pl.BlockSpec(memory_space=pl.ANY),
        ],
        out_specs=pl.BlockSpec((L, d), lambda n: (n, 0)),
        scratch_shapes=[
            pltpu.VMEM((L, d), BF16),
            pltpu.VMEM((L, d), BF16),
            pltpu.VMEM((L, n_in), F32),
            pltpu.VMEM((L, heads * dv), BF16),
            pltpu.VMEM((heads, dk, dv), F32),
            pltpu.VMEM((heads, L, L), F32),
            pltpu.VMEM((d, n_in), BF16),
            pltpu.VMEM((heads * dv, d), BF16),
            pltpu.VMEM((2,) + W_PIECE, F32),
            pltpu.SemaphoreType.DMA((2,)),
        ],
        compiler_params=pltpu.CompilerParams(
            dimension_semantics=("arbitrary",),
            vmem_limit_bytes=V7X_VMEM_LIMIT_BYTES),
        name="retention_mixer",
    )(x2, x2, mod, mod, g.reshape(1, d), cos, sin, w_in, gn_g, gn_b, w_out)
    return out.reshape(b, s, d)


def _rope_tables(seq, dk):
    pos = jnp.arange(seq, dtype=F32)
    inv = ROPE_BASE ** (-jnp.arange(0, dk, 2, dtype=F32) / dk)
    ang = pos[:, None] * inv[None, :]
    return jnp.cos(ang), jnp.sin(ang)


def kernel(x, c, ada_w, ada_b, norm_mix_g, norm_mlp_g, conv_w_pw1, conv_b_pw1, conv_w_dw, conv_b_dw, conv_ln_g, conv_ln_b, conv_w_pw2, conv_b_pw2, ret_w_in, ret_gn_g, ret_gn_b, ret_w_out, mlp_w1, mlp_w2, final_norm_g):
    depth = ada_w.shape[0]
    b, s, d = x.shape
    assert s % CONV_TM == 0 and s % MLP_TM == 0 and s % RET_L == 0 and RET_L % CHUNK == 0
    assert CONV_HALO >= CONV_WIDTH - 1 and CONV_HALO % V7X_SUBLANES == 0 and CONV_TM % CONV_ROWS == 0
    assert d == 2 * MLP_PHASES * V7X_LANES

    mod = _ada(c, ada_w, ada_b).reshape(depth, b, 6, d)
    cos, sin = _rope_tables(s, d // RET_HEADS)
    for i in range(depth):
        jm = i // 2
        if i % 2 == 0:
            assert i != depth - 1
            x = _conv_mlp_layer(x, mod[i], norm_mix_g[i], conv_w_pw1, conv_b_pw1[jm], conv_w_dw[jm], conv_b_dw[jm],
                                conv_ln_g[jm], conv_ln_b[jm], conv_w_pw2, conv_b_pw2[jm], norm_mlp_g[i],
                                mlp_w1, mlp_w2, conv_layer=jm, mlp_layer=i)
        else:
            x = _ret_layer(x, mod[i], norm_mix_g[i], cos, sin, ret_w_in, ret_gn_g[jm], ret_gn_b[jm], ret_w_out,
                           ret_layer=jm)
            x = _mlp_layer(x, mod[i], norm_mlp_g[i], mlp_w1, mlp_w2, i, final_norm_g, final=(i == depth - 1))
    return x
```

```python
import functools

import numpy as np
import jax
import jax.numpy as jnp
from jax import lax
from jax.experimental import pallas as pl
from jax.experimental.pallas import tpu as pltpu

F32 = jnp.float32
BF16 = jnp.bfloat16

EPS = 1e-6
CHUNK = 64
CONV_WIDTH = 31
RET_HEADS = 4
ROPE_BASE = 10000.0

V7X_SUBLANES = 8
V7X_LANES = 128
V7X_MXU_COLS = 256
V7X_VMEM_LIMIT_BYTES = 56 * 1024 * 1024

CONV_TM = 512
CONV_HALO = 32
CONV_ROWS = 32
MLP_PHASES = 4
MLP_TM = 1024
RET_L = 256


def _resident(shape, layer=None):
    zeros = (0,) * len(shape)
    if layer is None:
        return pl.BlockSpec(shape, lambda *_: zeros, pipeline_mode=pl.Buffered(1))
    return pl.BlockSpec((None,) + tuple(shape), lambda *_: (layer,) + zeros, pipeline_mode=pl.Buffered(1))


def _rmsnorm_mod(x, g, shift, scale):
    y = x * lax.rsqrt(jnp.mean(x * x, axis=-1, keepdims=True) + EPS)
    return (y * g) * (1.0 + scale) + shift


def _silu(v):
    return v * jax.nn.sigmoid(v)


def _after(value, dep):
    bits = pltpu.bitcast(dep, jnp.uint32)
    zero = lax.shift_right_logical(lax.shift_right_logical(bits, jnp.uint32(16)), jnp.uint32(16))
    return pltpu.bitcast(pltpu.bitcast(value, jnp.uint32) + zero, F32)


def _ada_kernel(c_ref, w_ref, b_ref, o_ref):
    cond = _silu(c_ref[...])
    o_ref[...] = jnp.dot(cond.astype(BF16), w_ref[...].astype(BF16),
                         preferred_element_type=F32) + b_ref[...]


def _ada(c, ada_w, ada_b):
    depth, d, n = ada_w.shape
    b = c.shape[0]
    tn = n // 4
    return pl.pallas_call(
        _ada_kernel,
        out_shape=jax.ShapeDtypeStruct((depth, b, n), F32),
        grid=(depth, n // tn),
        in_specs=[
            pl.BlockSpec((b, d), lambda l, j: (0, 0)),
            pl.BlockSpec((None, d, tn), lambda l, j: (l, 0, j)),
            pl.BlockSpec((None, 1, tn), lambda l, j: (l, 0, j)),
        ],
        out_specs=pl.BlockSpec((None, b, tn), lambda l, j: (l, 0, j)),
        compiler_params=pltpu.CompilerParams(
            dimension_semantics=("arbitrary", "arbitrary"),
            vmem_limit_bytes=V7X_VMEM_LIMIT_BYTES),
        name="ada_mod",
    )(c, ada_w, ada_b.reshape(depth, 1, n))


def _conv_mlp_kernel(x_ref, modc_ref, modp_ref, gmix_ref, w1_ref, b1_ref, wdw_ref, bdw_ref,
                     lng_ref, lnb_ref, w2_ref, b2_ref, gmlp_ref, m1_hbm, m2_hbm, o_ref,
                     ubuf, cbuf, x1buf, h2buf, tbuf, fbuf, m1_ref, m2_ref, wsem, *, tiles_per_seq, mlp_layer):
    tm, d = x_ref.shape
    groups = d // V7X_LANES
    fw = m1_ref.shape[2]
    ow = m2_ref.shape[2]
    n = pl.program_id(0)

    def weight_block_copies():
        copies = []
        for q in range(MLP_PHASES):
            copies.append(pltpu.make_async_copy(
                m1_hbm.at[mlp_layer, :, pl.ds(q * fw, fw)], m1_ref.at[q], wsem.at[q]))
            copies.append(pltpu.make_async_copy(
                m2_hbm.at[mlp_layer, :, pl.ds(q * ow, ow)], m2_ref.at[q], wsem.at[MLP_PHASES + q]))
        return copies

    @pl.when(n == 0)
    def _():
        for cp in weight_block_copies():
            cp.start()
        x1buf[...] = jnp.zeros(x1buf.shape, F32)
        for cp in weight_block_copies():
            cp.wait()

    @pl.when(n % tiles_per_seq == 0)
    def _():
        ubuf[:, 0:CONV_HALO, :] = jnp.zeros((groups, CONV_HALO, V7X_LANES), F32)

    @pl.when(n % tiles_per_seq != 0)
    def _():
        ubuf[:, 0:CONV_HALO, :] = ubuf[:, tm:tm + CONV_HALO, :]

    h = _rmsnorm_mod(x_ref[...], gmix_ref[...], modc_ref[0:1, :], modc_ref[1:2, :]).astype(BF16)
    for c0 in range(0, d, V7X_MXU_COLS):
        a = jnp.dot(h, w1_ref[:, c0:c0 + V7X_MXU_COLS], preferred_element_type=F32) + b1_ref[:, c0:c0 + V7X_MXU_COLS]
        gt = (jnp.dot(h, w1_ref[:, d + c0:d + c0 + V7X_MXU_COLS], preferred_element_type=F32)
              + b1_ref[:, d + c0:d + c0 + V7X_MXU_COLS])
        u = a * jax.nn.sigmoid(gt)
        for k in range(V7X_MXU_COLS // V7X_LANES):
            ubuf[c0 // V7X_LANES + k, CONV_HALO:CONV_HALO + tm, :] = u[:, k * V7X_LANES:(k + 1) * V7X_LANES]
    h2buf[...] = _rmsnorm_mod(x1buf[...], gmlp_ref[...], modp_ref[3:4, :], modp_ref[4:5, :]).astype(BF16)

    first_tap = CONV_HALO - (CONV_WIDTH - 1)

    def conv_taps(gi):
        done = None
        for r0 in range(0, tm, CONV_ROWS):
            win = ubuf.at[gi, r0:r0 + CONV_ROWS + CONV_HALO, :]
            tap = jnp.zeros((CONV_ROWS, V7X_LANES), F32) + bdw_ref[gi]
            if done is not None:
                tap = _after(tap, done)
            for t in range(CONV_WIDTH):
                tap = tap + win[pl.ds(first_tap + t, CONV_ROWS, stride=1), :] * wdw_ref[gi, t:t + 1, :]
            cbuf[gi, r0:r0 + CONV_ROWS, :] = tap
            done = tap

    def hidden_phase(p, carry):
        for c0 in range(0, fw, V7X_MXU_COLS):
            t1 = jnp.dot(h2buf[...], m1_ref[p, :, c0:c0 + V7X_MXU_COLS], preferred_element_type=F32)
            t1 = jnp.maximum(t1, 0.0)
            tbuf[p, :, c0:c0 + V7X_MXU_COLS] = (t1 * t1).astype(BF16)
        conv_taps(p)
        return carry

    lax.fori_loop(0, MLP_PHASES, hidden_phase, 0)

    def out_phase(p, carry):
        ff = jnp.dot(tbuf[0], m2_ref[p, 0:fw, :], preferred_element_type=F32)
        for q in range(1, MLP_PHASES):
            ff = ff + jnp.dot(tbuf[q], m2_ref[p, q * fw:(q + 1) * fw, :], preferred_element_type=F32)
        conv_taps(MLP_PHASES + p)
        fbuf[p] = ff
        return carry

    lax.fori_loop(0, MLP_PHASES, out_phase, 0)

    ff = jnp.concatenate([fbuf[p] for p in range(MLP_PHASES)], axis=-1)
    o_ref[...] = x1buf[...] + modp_ref[5:6, :] * ff

    cv = jnp.concatenate([cbuf[gi] for gi in range(groups)], axis=-1)
    mu = jnp.mean(cv, axis=-1, keepdims=True)
    cen = cv - mu
    var = jnp.mean(cen * cen, axis=-1, keepdims=True)
    v = cen * lax.rsqrt(var + EPS) * lng_ref[...] + lnb_ref[...]
    vb = _silu(v).astype(BF16)
    for c0 in range(0, d, V7X_MXU_COLS):
        cols = slice(c0, c0 + V7X_MXU_COLS)
        y = jnp.dot(vb, w2_ref[:, cols], preferred_element_type=F32) + b2_ref[:, cols]
        x1buf[:, cols] = x_ref[:, cols] + modc_ref[2:3, cols] * y


def _conv_mlp_layer(x, mod, gmix, w1, b1, wdw, bdw, lng, lnb, w2, b2, gmlp, m1, m2, conv_layer, mlp_layer):
    b, s, d = x.shape
    d_ff = m1.shape[-1]
    fw = d_ff // MLP_PHASES
    ow = d // MLP_PHASES
    tm = CONV_TM
    tiles_per_seq = s // tm
    n_tiles = b * tiles_per_seq
    groups = d // V7X_LANES
    row = lambda v: v.reshape(1, -1)
    cur = lambda n: jnp.minimum(n, n_tiles - 1)
    prev = lambda n: jnp.maximum(n - 1, 0)
    out = pl.pallas_call(
        functools.partial(_conv_mlp_kernel, tiles_per_seq=tiles_per_seq, mlp_layer=mlp_layer),
        out_shape=jax.ShapeDtypeStruct((b * s, d), F32),
        grid=(n_tiles + 1,),
        in_specs=[
            pl.BlockSpec((tm, d), lambda n: (cur(n), 0)),
            pl.BlockSpec((None, 6, d), lambda n: (cur(n) // tiles_per_seq, 0, 0)),
            pl.BlockSpec((None, 6, d), lambda n: (prev(n) // tiles_per_seq, 0, 0)),
            _resident((1, d)),
            _resident((d, 2 * d), conv_layer),
            _resident((1, 2 * d)),
            _resident((groups, CONV_WIDTH, V7X_LANES)),
            _resident((groups, 1, V7X_LANES)),
            _resident((1, d)),
            _resident((1, d)),
            _resident((d, d), conv_layer),
            _resident((1, d)),
            _resident((1, d)),
            pl.BlockSpec(memory_space=pl.ANY),
            pl.BlockSpec(memory_space=pl.ANY),
        ],
        out_specs=pl.BlockSpec((tm, d), lambda n: (prev(n), 0)),
        scratch_shapes=[
            pltpu.VMEM((groups, CONV_HALO + tm, V7X_LANES), F32),
            pltpu.VMEM((groups, tm, V7X_LANES), F32),
            pltpu.VMEM((tm, d), F32),
            pltpu.VMEM((tm, d), BF16),
            pltpu.VMEM((MLP_PHASES, tm, fw), BF16),
            pltpu.VMEM((MLP_PHASES, tm, ow), F32),
            pltpu.VMEM((MLP_PHASES, d, fw), BF16),
            pltpu.VMEM((MLP_PHASES, d_ff, ow), BF16),
            pltpu.SemaphoreType.DMA((2 * MLP_PHASES,)),
        ],
        compiler_params=pltpu.CompilerParams(
            dimension_semantics=("arbitrary",),
            vmem_limit_bytes=V7X_VMEM_LIMIT_BYTES),
        name="conv_mlp",
    )(x.reshape(b * s, d), mod, mod, row(gmix), w1, row(b1),
      wdw.reshape(CONV_WIDTH, groups, V7X_LANES).transpose(1, 0, 2), bdw.reshape(groups, 1, V7X_LANES),
      row(lng), row(lnb), w2, row(b2), row(gmlp), m1, m2)
    return out.reshape(b, s, d)


def _mlp_kernel(x_ref, mod_ref, g_ref, w1_ref, w2_ref, fg_ref, o_ref, *, final):
    d_ff = w1_ref.shape[1]
    fw = d_ff // MLP_PHASES
    x = x_ref[...]
    h = _rmsnorm_mod(x, g_ref[...], mod_ref[3:4, :], mod_ref[4:5, :]).astype(BF16)
    acc = jnp.zeros(x.shape, F32)
    for c0 in range(0, d_ff, fw):
        t = jnp.maximum(jnp.dot(h, w1_ref[:, c0:c0 + fw], preferred_element_type=F32), 0.0)
        acc = acc + jnp.dot((t * t).astype(BF16), w2_ref[c0:c0 + fw, :], preferred_element_type=F32)
    y = x + mod_ref[5:6, :] * acc
    if final:
        y = y * lax.rsqrt(jnp.mean(y * y, axis=-1, keepdims=True) + EPS) * fg_ref[...]
    o_ref[...] = y


def _mlp_layer(x, mod, g, w1, w2, layer, fg, final):
    b, s, d = x.shape
    d_ff = w1.shape[-1]
    tm = MLP_TM
    return pl.pallas_call(
        functools.partial(_mlp_kernel, final=final),
        out_shape=jax.ShapeDtypeStruct(x.shape, F32),
        grid=(b, s // tm),
        in_specs=[
            pl.BlockSpec((None, tm, d), lambda bi, j: (bi, j, 0)),
            pl.BlockSpec((None, 6, d), lambda bi, j: (bi, 0, 0)),
            _resident((1, d)),
            _resident((d, d_ff), layer),
            _resident((d_ff, d), layer),
            _resident((1, d)),
        ],
        out_specs=pl.BlockSpec((None, tm, d), lambda bi, j: (bi, j, 0)),
        compiler_params=pltpu.CompilerParams(
            dimension_semantics=("arbitrary", "arbitrary"),
            vmem_limit_bytes=V7X_VMEM_LIMIT_BYTES),
        name="mlp_final" if final else "mlp",
    )(x, mod, g.reshape(1, d), w1, w2, fg.reshape(1, d))


def _log_gamma(head):
    return float(np.log(np.float32(1.0) - np.float32(2.0) ** np.float32(-5.0 - head)))


def _ret_kernel(x_ref, xnext_ref, mod_ref, modnext_ref, g_ref, cos_ref, sin_ref, win_ref, gng_ref, gnb_ref,
                wout_ref, o_ref, hbuf, hnext, proj, ybuf, state, dmask, *, tiles_per_seq):
    L, d = x_ref.shape
    heads = RET_HEADS
    dk = d // heads
    dv = 2 * d // heads
    half = dk // 2
    n = pl.program_id(0)

    @pl.when(n == 0)
    def _():
        hnext[...] = _rmsnorm_mod(x_ref[...], g_ref[...], mod_ref[0:1, :], mod_ref[1:2, :]).astype(BF16)
        r = lax.broadcasted_iota(jnp.int32, (L, L), 0)
        c = lax.broadcasted_iota(jnp.int32, (L, L), 1)
        dist = jnp.abs(r - c).astype(F32)
        chunk_shift = CHUNK.bit_length() - 1
        visible = jnp.right_shift(c, chunk_shift) <= jnp.right_shift(r, chunk_shift)
        for hd in range(heads):
            dmask[hd] = jnp.where(visible, jnp.exp(_log_gamma(hd) * dist), 0.0)

    @pl.when(n % tiles_per_seq == 0)
    def _():
        state[...] = jnp.zeros(state.shape, F32)

    hbuf[...] = hnext[...]
    x = x_ref[...]
    h = hbuf[...]
    n_in = win_ref.shape[1]
    for c0 in range(0, n_in, d):
        proj[:, c0:c0 + d] = jnp.dot(h, win_ref[:, c0:c0 + d], preferred_element_type=F32)

    hn = _rmsnorm_mod(xnext_ref[...], g_ref[...], modnext_ref[0:1, :], modnext_ref[1:2, :]).astype(BF16)
    hnext[...] = hn

    cos = cos_ref[...]
    sin = sin_ref[...]
    last = hn[L - 2 * V7X_SUBLANES:L, d - V7X_LANES:d].astype(F32)[0:V7X_SUBLANES, :]
    cos = jnp.concatenate([_after(cos[0:V7X_SUBLANES, :], last), cos[V7X_SUBLANES:, :]], axis=0)
    idx = lax.broadcasted_iota(jnp.int32, (L, 1), 0).astype(F32)
    k_off, v_off, g_off = d, 2 * d, 2 * d + heads * dv

    def rope(base):
        x1 = proj[:, base:base + half]
        x2 = proj[:, base + half:base + dk]
        return jnp.concatenate([x1 * cos - x2 * sin, x2 * cos + x1 * sin], axis=-1)

    for hd in range(heads):
        lg = _log_gamma(hd)
        q = rope(hd * dk)
        k = rope(k_off + hd * dk) * (dk ** -0.5)
        vb = proj[:, v_off + hd * dv:v_off + (hd + 1) * dv].astype(BF16)
        qb = q.astype(BF16)
        scores = lax.dot_general(qb, k.astype(BF16), (((1,), (1,)), ((), ())),
                                 preferred_element_type=F32) * dmask[hd]
        intra = jnp.dot(scores.astype(BF16), vb, preferred_element_type=F32)
        xi = jnp.exp(lg * (idx + 1.0))
        st = state[hd]
        cross = jnp.dot(qb, st.astype(BF16), preferred_element_type=F32) * xi
        zeta = jnp.exp(lg * (float(L - 1) - idx))
        kz = (k * zeta).astype(BF16)
        state[hd] = st * float(np.exp(np.float32(lg) * np.float32(L))) + lax.dot_general(
            kz, vb, (((0,), (0,)), ((), ())), preferred_element_type=F32)
        y = intra + cross
        mu = jnp.mean(y, axis=-1, keepdims=True)
        cen = y - mu
        var = jnp.mean(cen * cen, axis=-1, keepdims=True)
        yn = cen * lax.rsqrt(var + EPS) * gng_ref[hd:hd + 1, :] + gnb_ref[hd:hd + 1, :]
        gate = proj[:, g_off + hd * dv:g_off + (hd + 1) * dv]
        ybuf[:, hd * dv:(hd + 1) * dv] = (_silu(gate) * yn).astype(BF16)

    out = jnp.dot(ybuf[...], wout_ref[...], preferred_element_type=F32)
    o_ref[...] = x + mod_ref[2:3, :] * out


def _ret_layer(x, mod, g, cos, sin, w_in, gn_g, gn_b, w_out, ret_layer):
    b, s, d = x.shape
    L = RET_L
    heads = RET_HEADS
    dk, dv = d // heads, 2 * d // heads
    n_in = w_in.shape[-1]
    tiles_per_seq = s // L
    n_tiles = b * tiles_per_seq
    nxt = lambda n: jnp.minimum(n + 1, n_tiles - 1)
    x2 = x.reshape(b * s, d)
    out = pl.pallas_call(
        functools.partial(_ret_kernel, tiles_per_seq=tiles_per_seq),
        out_shape=jax.ShapeDtypeStruct((b * s, d), F32),
        grid=(n_tiles,),
        in_specs=[
            pl.BlockSpec((L, d), lambda n: (n, 0)),
            pl.BlockSpec((L, d), lambda n: (nxt(n), 0)),
            pl.BlockSpec((None, 6, d), lambda n: (n // tiles_per_seq, 0, 0)),
            pl.BlockSpec((None, 6, d), lambda n: (nxt(n) // tiles_per_seq, 0, 0)),
            _resident((1, d)),
            pl.BlockSpec((L, dk // 2), lambda n: (n % tiles_per_seq, 0)),
            pl.BlockSpec((L, dk // 2), lambda n: (n % tiles_per_seq, 0)),
            _resident((d, n_in), ret_layer),
            _resident((heads, dv)),
            _resident((heads, dv)),
            _resident((heads * dv, d), ret_layer),
        ],
        out_specs=pl.BlockSpec((L, d), lambda n: (n, 0)),
        scratch_shapes=[
            pltpu.VMEM((L, d), BF16),
            pltpu.VMEM((L, d), BF16),
            pltpu.VMEM((L, n_in), F32),
            pltpu.VMEM((L, heads * dv), BF16),
            pltpu.VMEM((heads, dk, dv), F32),
            pltpu.VMEM((heads, L, L), F32),
        ],
        compiler_params=pltpu.CompilerParams(
            dimension_semantics=("arbitrary",),
            vmem_limit_bytes=V7X_VMEM_LIMIT_BYTES),
        name="retention_mixer",
    )(x2, x2, mod, mod, g.reshape(1, d), cos, sin, w_in, gn_g, gn_b, w_out)
    return out.reshape(b, s, d)


def _rope_tables(seq, dk):
    pos = jnp.arange(seq, dtype=F32)
    inv = ROPE_BASE ** (-jnp.arange(0, dk, 2, dtype=F32) / dk)
    ang = pos[:, None] * inv[None, :]
    return jnp.cos(ang), jnp.sin(ang)


def kernel(x, c, ada_w, ada_b, norm_mix_g, norm_mlp_g, conv_w_pw1, conv_b_pw1, conv_w_dw, conv_b_dw, conv_ln_g, conv_ln_b, conv_w_pw2, conv_b_pw2, ret_w_in, ret_gn_g, ret_gn_b, ret_w_out, mlp_w1, mlp_w2, final_norm_g):
    depth = ada_w.shape[0]
    b, s, d = x.shape
    assert s % CONV_TM == 0 and s % MLP_TM == 0 and s % RET_L == 0 and RET_L % CHUNK == 0
    assert CONV_HALO >= CONV_WIDTH - 1 and CONV_HALO % V7X_SUBLANES == 0 and CONV_TM % CONV_ROWS == 0
    assert d == 2 * MLP_PHASES * V7X_LANES

    mod = _ada(c, ada_w, ada_b).reshape(depth, b, 6, d)
    cos, sin = _rope_tables(s, d // RET_HEADS)
    pw1, pw2 = conv_w_pw1.astype(BF16), conv_w_pw2.astype(BF16)
    w_in, w_out = ret_w_in.astype(BF16), ret_w_out.astype(BF16)
    m1, m2 = mlp_w1.astype(BF16), mlp_w2.astype(BF16)
    for i in range(depth):
        jm = i // 2
        if i % 2 == 0:
            assert i != depth - 1
            x = _conv_mlp_layer(x, mod[i], norm_mix_g[i], pw1, conv_b_pw1[jm], conv_w_dw[jm], conv_b_dw[jm],
                                conv_ln_g[jm], conv_ln_b[jm], pw2, conv_b_pw2[jm], norm_mlp_g[i],
                                m1, m2, conv_layer=jm, mlp_layer=i)
        else:
            x = _ret_layer(x, mod[i], norm_mix_g[i], cos, sin, w_in, ret_gn_g[jm], ret_gn_b[jm], w_out, ret_layer=jm)
            x = _mlp_layer(x, mod[i], norm_mlp_g[i], m1, m2, i, final_norm_g, final=(i == depth - 1))
    return x
```

```python
import functools

import numpy as np
import jax
import jax.numpy as jnp
from jax import lax
from jax.experimental import pallas as pl
from jax.experimental.pallas import tpu as pltpu

F32 = jnp.float32
BF16 = jnp.bfloat16

EPS = 1e-6
CHUNK = 64
CONV_WIDTH = 31
RET_HEADS = 4
ROPE_BASE = 10000.0

V7X_SUBLANES = 8
V7X_LANES = 128
V7X_MXU_COLS = 256
V7X_VMEM_LIMIT_BYTES = 56 * 1024 * 1024

CONV_TM = 512
CONV_HALO = 32
CONV_ROWS = 32
MLP_PHASES = 4
MLP_TM = 1024
RET_L = 256


def _resident(shape, layer=None):
    zeros = (0,) * len(shape)
    if layer is None:
        return pl.BlockSpec(shape, lambda *_: zeros, pipeline_mode=pl.Buffered(1))
    return pl.BlockSpec((None,) + tuple(shape), lambda *_: (layer,) + zeros, pipeline_mode=pl.Buffered(1))


def _rmsnorm_mod(x, g, shift, scale):
    y = x * lax.rsqrt(jnp.mean(x * x, axis=-1, keepdims=True) + EPS)
    return (y * g) * (1.0 + scale) + shift


def _silu(v):
    return v * jax.nn.sigmoid(v)


def _after(value, dep):
    bits = pltpu.bitcast(dep, jnp.uint32)
    zero = lax.shift_right_logical(lax.shift_right_logical(bits, jnp.uint32(16)), jnp.uint32(16))
    return pltpu.bitcast(pltpu.bitcast(value, jnp.uint32) + zero, F32)


def _ada_kernel(c_ref, w_ref, b_ref, o_ref):
    cond = _silu(c_ref[...])
    o_ref[...] = jnp.dot(cond.astype(BF16), w_ref[...].astype(BF16),
                         preferred_element_type=F32) + b_ref[...]


def _ada(c, ada_w, ada_b):
    depth, d, n = ada_w.shape
    b = c.shape[0]
    tn = n // 4
    return pl.pallas_call(
        _ada_kernel,
        out_shape=jax.ShapeDtypeStruct((depth, b, n), F32),
        grid=(depth, n // tn),
        in_specs=[
            pl.BlockSpec((b, d), lambda l, j: (0, 0)),
            pl.BlockSpec((None, d, tn), lambda l, j: (l, 0, j)),
            pl.BlockSpec((None, 1, tn), lambda l, j: (l, 0, j)),
        ],
        out_specs=pl.BlockSpec((None, b, tn), lambda l, j: (l, 0, j)),
        compiler_params=pltpu.CompilerParams(
            dimension_semantics=("arbitrary", "arbitrary"),
            vmem_limit_bytes=V7X_VMEM_LIMIT_BYTES),
        name="ada_mod",
    )(c, ada_w, ada_b.reshape(depth, 1, n))


def _conv_mlp_kernel(x_ref, xnext_ref, modc_ref, modnext_ref, modp_ref, gmix_ref, w1_ref, b1_ref, wdw_ref, bdw_ref,
                     lng_ref, lnb_ref, w2_ref, b2_ref, gmlp_ref, m1_hbm, m2_hbm, o_ref,
                     ubuf, cbuf, x1buf, h2buf, tbuf, fbuf, m1_ref, m2_ref, wsem, *, tiles_per_seq, mlp_layer):
    tm, d = x_ref.shape
    groups = d // V7X_LANES
    fw = m1_ref.shape[2]
    ow = m2_ref.shape[2]
    n = pl.program_id(0)

    def weight_block_copies():
        copies = []
        for q in range(MLP_PHASES):
            copies.append(pltpu.make_async_copy(
                m1_hbm.at[mlp_layer, :, pl.ds(q * fw, fw)], m1_ref.at[q], wsem.at[q]))
            copies.append(pltpu.make_async_copy(
                m2_hbm.at[mlp_layer, :, pl.ds(q * ow, ow)], m2_ref.at[q], wsem.at[MLP_PHASES + q]))
        return copies

    def pointwise1(xt_ref, mod_ref):
        h = _rmsnorm_mod(xt_ref[...], gmix_ref[...], mod_ref[0:1, :], mod_ref[1:2, :]).astype(BF16)
        for c0 in range(0, d, V7X_MXU_COLS):
            a = (jnp.dot(h, w1_ref[:, c0:c0 + V7X_MXU_COLS], preferred_element_type=F32)
                 + b1_ref[:, c0:c0 + V7X_MXU_COLS])
            gt = (jnp.dot(h, w1_ref[:, d + c0:d + c0 + V7X_MXU_COLS], preferred_element_type=F32)
                  + b1_ref[:, d + c0:d + c0 + V7X_MXU_COLS])
            u = a * jax.nn.sigmoid(gt)
            for k in range(V7X_MXU_COLS // V7X_LANES):
                ubuf[c0 // V7X_LANES + k, CONV_HALO:CONV_HALO + tm, :] = u[:, k * V7X_LANES:(k + 1) * V7X_LANES]

    @pl.when(n == 0)
    def _():
        for cp in weight_block_copies():
            cp.start()
        x1buf[...] = jnp.zeros(x1buf.shape, F32)
        h2buf[...] = jnp.zeros(h2buf.shape, BF16)
        ubuf[:, 0:CONV_HALO, :] = jnp.zeros((groups, CONV_HALO, V7X_LANES), F32)
        pointwise1(x_ref, modc_ref)
        for cp in weight_block_copies():
            cp.wait()

    first_tap = CONV_HALO - (CONV_WIDTH - 1)

    def conv_taps(gi):
        done = None
        for r0 in range(0, tm, CONV_ROWS):
            win = ubuf.at[gi, r0:r0 + CONV_ROWS + CONV_HALO, :]
            tap = jnp.zeros((CONV_ROWS, V7X_LANES), F32) + bdw_ref[gi]
            if done is not None:
                tap = _after(tap, done)
            for t in range(CONV_WIDTH):
                tap = tap + win[pl.ds(first_tap + t, CONV_ROWS, stride=1), :] * wdw_ref[gi, t:t + 1, :]
            cbuf[gi, r0:r0 + CONV_ROWS, :] = tap
            done = tap

    def hidden_phase(p, carry):
        for c0 in range(0, fw, V7X_MXU_COLS):
            t1 = jnp.dot(h2buf[...], m1_ref[p, :, c0:c0 + V7X_MXU_COLS], preferred_element_type=F32)
            t1 = jnp.maximum(t1, 0.0)
            tbuf[p, :, c0:c0 + V7X_MXU_COLS] = (t1 * t1).astype(BF16)
        conv_taps(p)
        return carry

    lax.fori_loop(0, MLP_PHASES, hidden_phase, 0)

    def out_phase(p, carry):
        ff = jnp.dot(tbuf[0], m2_ref[p, 0:fw, :], preferred_element_type=F32)
        for q in range(1, MLP_PHASES):
            ff = ff + jnp.dot(tbuf[q], m2_ref[p, q * fw:(q + 1) * fw, :], preferred_element_type=F32)
        conv_taps(MLP_PHASES + p)
        fbuf[p] = ff
        return carry

    lax.fori_loop(0, MLP_PHASES, out_phase, 0)

    @pl.when((n + 1) % tiles_per_seq == 0)
    def _():
        ubuf[:, 0:CONV_HALO, :] = jnp.zeros((groups, CONV_HALO, V7X_LANES), F32)

    @pl.when((n + 1) % tiles_per_seq != 0)
    def _():
        ubuf[:, 0:CONV_HALO, :] = ubuf[:, tm:tm + CONV_HALO, :]

    ff = jnp.concatenate([fbuf[p] for p in range(MLP_PHASES)], axis=-1)
    o_ref[...] = x1buf[...] + modp_ref[5:6, :] * ff

    cv = jnp.concatenate([cbuf[gi] for gi in range(groups)], axis=-1)
    mu = jnp.mean(cv, axis=-1, keepdims=True)
    cen = cv - mu
    var = jnp.mean(cen * cen, axis=-1, keepdims=True)
    v = cen * lax.rsqrt(var + EPS) * lng_ref[...] + lnb_ref[...]
    vb = _silu(v).astype(BF16)
    for c0 in range(0, d, V7X_MXU_COLS):
        cols = slice(c0, c0 + V7X_MXU_COLS)
        y = jnp.dot(vb, w2_ref[:, cols], preferred_element_type=F32) + b2_ref[:, cols]
        x1buf[:, cols] = x_ref[:, cols] + modc_ref[2:3, cols] * y
    h2buf[...] = _rmsnorm_mod(x1buf[...], gmlp_ref[...], modc_ref[3:4, :], modc_ref[4:5, :]).astype(BF16)
    pointwise1(xnext_ref, modnext_ref)


def _conv_mlp_layer(x, mod, gmix, w1, b1, wdw, bdw, lng, lnb, w2, b2, gmlp, m1, m2, conv_layer, mlp_layer):
    b, s, d = x.shape
    d_ff = m1.shape[-1]
    fw = d_ff // MLP_PHASES
    ow = d // MLP_PHASES
    tm = CONV_TM
    tiles_per_seq = s // tm
    n_tiles = b * tiles_per_seq
    groups = d // V7X_LANES
    row = lambda v: v.reshape(1, -1)
    cur = lambda n: jnp.minimum(n, n_tiles - 1)
    nxt = lambda n: jnp.minimum(n + 1, n_tiles - 1)
    prev = lambda n: jnp.maximum(n - 1, 0)
    out = pl.pallas_call(
        functools.partial(_conv_mlp_kernel, tiles_per_seq=tiles_per_seq, mlp_layer=mlp_layer),
        out_shape=jax.ShapeDtypeStruct((b * s, d), F32),
        grid=(n_tiles + 1,),
        in_specs=[
            pl.BlockSpec((tm, d), lambda n: (cur(n), 0)),
            pl.BlockSpec((tm, d), lambda n: (nxt(n), 0)),
            pl.BlockSpec((None, 6, d), lambda n: (cur(n) // tiles_per_seq, 0, 0)),
            pl.BlockSpec((None, 6, d), lambda n: (nxt(n) // tiles_per_seq, 0, 0)),
            pl.BlockSpec((None, 6, d), lambda n: (prev(n) // tiles_per_seq, 0, 0)),
            _resident((1, d)),
            _resident((d, 2 * d), conv_layer),
            _resident((1, 2 * d)),
            _resident((groups, CONV_WIDTH, V7X_LANES)),
            _resident((groups, 1, V7X_LANES)),
            _resident((1, d)),
            _resident((1, d)),
            _resident((d, d), conv_layer),
            _resident((1, d)),
            _resident((1, d)),
            pl.BlockSpec(memory_space=pl.ANY),
            pl.BlockSpec(memory_space=pl.ANY),
        ],
        out_specs=pl.BlockSpec((tm, d), lambda n: (prev(n), 0)),
        scratch_shapes=[
            pltpu.VMEM((groups, CONV_HALO + tm, V7X_LANES), F32),
            pltpu.VMEM((groups, tm, V7X_LANES), F32),
            pltpu.VMEM((tm, d), F32),
            pltpu.VMEM((tm, d), BF16),
            pltpu.VMEM((MLP_PHASES, tm, fw), BF16),
            pltpu.VMEM((MLP_PHASES, tm, ow), F32),
            pltpu.VMEM((MLP_PHASES, d, fw), BF16),
            pltpu.VMEM((MLP_PHASES, d_ff, ow), BF16),
            pltpu.SemaphoreType.DMA((2 * MLP_PHASES,)),
        ],
        compiler_params=pltpu.CompilerParams(
            dimension_semantics=("arbitrary",),
            vmem_limit_bytes=V7X_VMEM_LIMIT_BYTES),
        name="conv_mlp",
    )(x.reshape(b * s, d), x.reshape(b * s, d), mod, mod, mod, row(gmix), w1, row(b1),
      wdw.reshape(CONV_WIDTH, groups, V7X_LANES).transpose(1, 0, 2), bdw.reshape(groups, 1, V7X_LANES),
      row(lng), row(lnb), w2, row(b2), row(gmlp), m1, m2)
    return out.reshape(b, s, d)


def _mlp_kernel(x_ref, mod_ref, g_ref, w1_ref, w2_ref, fg_ref, o_ref, *, final):
    d_ff = w1_ref.shape[1]
    fw = d_ff // MLP_PHASES
    x = x_ref[...]
    h = _rmsnorm_mod(x, g_ref[...], mod_ref[3:4, :], mod_ref[4:5, :]).astype(BF16)
    acc = jnp.zeros(x.shape, F32)
    for c0 in range(0, d_ff, fw):
        t = jnp.maximum(jnp.dot(h, w1_ref[:, c0:c0 + fw], preferred_element_type=F32), 0.0)
        acc = acc + jnp.dot((t * t).astype(BF16), w2_ref[c0:c0 + fw, :], preferred_element_type=F32)
    y = x + mod_ref[5:6, :] * acc
    if final:
        y = y * lax.rsqrt(jnp.mean(y * y, axis=-1, keepdims=True) + EPS) * fg_ref[...]
    o_ref[...] = y


def _mlp_layer(x, mod, g, w1, w2, layer, fg, final):
    b, s, d = x.shape
    d_ff = w1.shape[-1]
    tm = MLP_TM
    return pl.pallas_call(
        functools.partial(_mlp_kernel, final=final),
        out_shape=jax.ShapeDtypeStruct(x.shape, F32),
        grid=(b, s // tm),
        in_specs=[
            pl.BlockSpec((None, tm, d), lambda bi, j: (bi, j, 0)),
            pl.BlockSpec((None, 6, d), lambda bi, j: (bi, 0, 0)),
            _resident((1, d)),
            _resident((d, d_ff), layer),
            _resident((d_ff, d), layer),
            _resident((1, d)),
        ],
        out_specs=pl.BlockSpec((None, tm, d), lambda bi, j: (bi, j, 0)),
        compiler_params=pltpu.CompilerParams(
            dimension_semantics=("arbitrary", "arbitrary"),
            vmem_limit_bytes=V7X_VMEM_LIMIT_BYTES),
        name="mlp_final" if final else "mlp",
    )(x, mod, g.reshape(1, d), w1, w2, fg.reshape(1, d))


def _log_gamma(head):
    return float(np.log(np.float32(1.0) - np.float32(2.0) ** np.float32(-5.0 - head)))


def _ret_kernel(x_ref, xnext_ref, mod_ref, modnext_ref, g_ref, cos_ref, sin_ref, win_ref, gng_ref, gnb_ref,
                wout_ref, o_ref, hbuf, hnext, proj, ybuf, state, dmask, *, tiles_per_seq):
    L, d = x_ref.shape
    heads = RET_HEADS
    dk = d // heads
    dv = 2 * d // heads
    half = dk // 2
    n = pl.program_id(0)

    @pl.when(n == 0)
    def _():
        hnext[...] = _rmsnorm_mod(x_ref[...], g_ref[...], mod_ref[0:1, :], mod_ref[1:2, :]).astype(BF16)
        r = lax.broadcasted_iota(jnp.int32, (L, L), 0)
        c = lax.broadcasted_iota(jnp.int32, (L, L), 1)
        dist = jnp.abs(r - c).astype(F32)
        chunk_shift = CHUNK.bit_length() - 1
        visible = jnp.right_shift(c, chunk_shift) <= jnp.right_shift(r, chunk_shift)
        for hd in range(heads):
            dmask[hd] = jnp.where(visible, jnp.exp(_log_gamma(hd) * dist), 0.0)

    @pl.when(n % tiles_per_seq == 0)
    def _():
        state[...] = jnp.zeros(state.shape, F32)

    hbuf[...] = hnext[...]
    x = x_ref[...]
    h = hbuf[...]
    n_in = win_ref.shape[1]
    for c0 in range(0, n_in, d):
        proj[:, c0:c0 + d] = jnp.dot(h, win_ref[:, c0:c0 + d], preferred_element_type=F32)

    hn = _rmsnorm_mod(xnext_ref[...], g_ref[...], modnext_ref[0:1, :], modnext_ref[1:2, :]).astype(BF16)
    hnext[...] = hn

    cos = cos_ref[...]
    sin = sin_ref[...]
    last = hn[L - 2 * V7X_SUBLANES:L, d - V7X_LANES:d].astype(F32)[0:V7X_SUBLANES, :]
    cos = jnp.concatenate([_after(cos[0:V7X_SUBLANES, :], last), cos[V7X_SUBLANES:, :]], axis=0)
    idx = lax.broadcasted_iota(jnp.int32, (L, 1), 0).astype(F32)
    k_off, v_off, g_off = d, 2 * d, 2 * d + heads * dv

    def rope(base):
        x1 = proj[:, base:base + half]
        x2 = proj[:, base + half:base + dk]
        return jnp.concatenate([x1 * cos - x2 * sin, x2 * cos + x1 * sin], axis=-1)

    for hd in range(heads):
        lg = _log_gamma(hd)
        q = rope(hd * dk)
        k = rope(k_off + hd * dk) * (dk ** -0.5)
        vb = proj[:, v_off + hd * dv:v_off + (hd + 1) * dv].astype(BF16)
        qb = q.astype(BF16)
        scores = lax.dot_general(qb, k.astype(BF16), (((1,), (1,)), ((), ())),
                                 preferred_element_type=F32) * dmask[hd]
        intra = jnp.dot(scores.astype(BF16), vb, preferred_element_type=F32)
        xi = jnp.exp(lg * (idx + 1.0))
        st = state[hd]
        cross = jnp.dot(qb, st.astype(BF16), preferred_element_type=F32) * xi
        zeta = jnp.exp(lg * (float(L - 1) - idx))
        kz = (k * zeta).astype(BF16)
        state[hd] = st * float(np.exp(np.float32(lg) * np.float32(L))) + lax.dot_general(
            kz, vb, (((0,), (0,)), ((), ())), preferred_element_type=F32)
        y = intra + cross
        mu = jnp.mean(y, axis=-1, keepdims=True)
        cen = y - mu
        var = jnp.mean(cen * cen, axis=-1, keepdims=True)
        yn = cen * lax.rsqrt(var + EPS) * gng_ref[hd:hd + 1, :] + gnb_ref[hd:hd + 1, :]
        gate = proj[:, g_off + hd * dv:g_off + (hd + 1) * dv]
        ybuf[:, hd * dv:(hd + 1) * dv] = (_silu(gate) * yn).astype(BF16)

    out = jnp.dot(ybuf[...], wout_ref[...], preferred_element_type=F32)
    o_ref[...] = x + mod_ref[2:3, :] * out


def _ret_layer(x, mod, g, cos, sin, w_in, gn_g, gn_b, w_out, ret_layer):
    b, s, d = x.shape
    L = RET_L
    heads = RET_HEADS
    dk, dv = d // heads, 2 * d // heads
    n_in = w_in.shape[-1]
    tiles_per_seq = s // L
    n_tiles = b * tiles_per_seq
    nxt = lambda n: jnp.minimum(n + 1, n_tiles - 1)
    x2 = x.reshape(b * s, d)
    out = pl.pallas_call(
        functools.partial(_ret_kernel, tiles_per_seq=tiles_per_seq),
        out_shape=jax.ShapeDtypeStruct((b * s, d), F32),
        grid=(n_tiles,),
        in_specs=[
            pl.BlockSpec((L, d), lambda n: (n, 0)),
            pl.BlockSpec((L, d), lambda n: (nxt(n), 0)),
            pl.BlockSpec((None, 6, d), lambda n: (n // tiles_per_seq, 0, 0)),
            pl.BlockSpec((None, 6, d), lambda n: (nxt(n) // tiles_per_seq, 0, 0)),
            _resident((1, d)),
            pl.BlockSpec((L, dk // 2), lambda n: (n % tiles_per_seq, 0)),
            pl.BlockSpec((L, dk // 2), lambda n: (n % tiles_per_seq, 0)),
            _resident((d, n_in), ret_layer),
            _resident((heads, dv)),
            _resident((heads, dv)),
            _resident((heads * dv, d), ret_layer),
        ],
        out_specs=pl.BlockSpec((L, d), lambda n: (n, 0)),
        scratch_shapes=[
            pltpu.VMEM((L, d), BF16),
            pltpu.VMEM((L, d), BF16),
            pltpu.VMEM((L, n_in), F32),
            pltpu.VMEM((L, heads * dv), BF16),
            pltpu.VMEM((heads, dk, dv), F32),
            pltpu.VMEM((heads, L, L), F32),
        ],
        compiler_params=pltpu.CompilerParams(
            dimension_semantics=("arbitrary",),
            vmem_limit_bytes=V7X_VMEM_LIMIT_BYTES),
        name="retention_mixer",
    )(x2, x2, mod, mod, g.reshape(1, d), cos, sin, w_in, gn_g, gn_b, w_out)
    return out.reshape(b, s, d)


def _rope_tables(seq, dk):
    pos = jnp.arange(seq, dtype=F32)
    inv = ROPE_BASE ** (-jnp.arange(0, dk, 2, dtype=F32) / dk)
    ang = pos[:, None] * inv[None, :]
    return jnp.cos(ang), jnp.sin(ang)


def kernel(x, c, ada_w, ada_b, norm_mix_g, norm_mlp_g, conv_w_pw1, conv_b_pw1, conv_w_dw, conv_b_dw, conv_ln_g, conv_ln_b, conv_w_pw2, conv_b_pw2, ret_w_in, ret_gn_g, ret_gn_b, ret_w_out, mlp_w1, mlp_w2, final_norm_g):
    depth = ada_w.shape[0]
    b, s, d = x.shape
    assert s % CONV_TM == 0 and s % MLP_TM == 0 and s % RET_L == 0 and RET_L % CHUNK == 0
    assert CONV_HALO >= CONV_WIDTH - 1 and CONV_HALO % V7X_SUBLANES == 0 and CONV_TM % CONV_ROWS == 0
    assert d == 2 * MLP_PHASES * V7X_LANES

    mod = _ada(c, ada_w, ada_b).reshape(depth, b, 6, d)
    cos, sin = _rope_tables(s, d // RET_HEADS)
    pw1, pw2 = conv_w_pw1.astype(BF16), conv_w_pw2.astype(BF16)
    w_in, w_out = ret_w_in.astype(BF16), ret_w_out.astype(BF16)
    m1, m2 = mlp_w1.astype(BF16), mlp_w2.astype(BF16)
    for i in range(depth):
        jm = i // 2
        if i % 2 == 0:
            assert i != depth - 1
            x = _conv_mlp_layer(x, mod[i], norm_mix_g[i], pw1, conv_b_pw1[jm], conv_w_dw[jm], conv_b_dw[jm],
                                conv_ln_g[jm], conv_ln_b[jm], pw2, conv_b_pw2[jm], norm_mlp_g[i],
                                m1, m2, conv_layer=jm, mlp_layer=i)
        else:
            x = _ret_layer(x, mod[i], norm_mix_g[i], cos, sin, w_in, ret_gn_g[jm], ret_gn_b[jm], w_out, ret_layer=jm)
            x = _mlp_layer(x, mod[i], norm_mlp_g[i], m1, m2, i, final_norm_g, final=(i == depth - 1))
    return x
```

```python
import functools

import numpy as np
import jax
import jax.numpy as jnp
from jax import lax
from jax.experimental import pallas as pl
from jax.experimental.pallas import tpu as pltpu

F32 = jnp.float32
BF16 = jnp.bfloat16

EPS = 1e-6
CHUNK = 64
CONV_WIDTH = 31
RET_HEADS = 4
ROPE_BASE = 10000.0

V7X_SUBLANES = 8
V7X_LANES = 128
V7X_MXU_COLS = 256
V7X_VMEM_LIMIT_BYTES = 56 * 1024 * 1024

CONV_TM = 512
CONV_HALO = 32
CONV_ROWS = 32
MLP_PHASES = 4
MLP_TM = 1024
RET_L = 256


def _resident(shape, layer=None):
    zeros = (0,) * len(shape)
    if layer is None:
        return pl.BlockSpec(shape, lambda *_: zeros, pipeline_mode=pl.Buffered(1))
    return pl.BlockSpec((None,) + tuple(shape), lambda *_: (layer,) + zeros, pipeline_mode=pl.Buffered(1))


def _rmsnorm_mod(x, g, shift, scale):
    y = x * lax.rsqrt(jnp.mean(x * x, axis=-1, keepdims=True) + EPS)
    return (y * g) * (1.0 + scale) + shift


def _silu(v):
    return v * jax.nn.sigmoid(v)


def _after(value, dep):
    bits = pltpu.bitcast(dep, jnp.uint32)
    zero = lax.shift_right_logical(lax.shift_right_logical(bits, jnp.uint32(16)), jnp.uint32(16))
    return pltpu.bitcast(pltpu.bitcast(value, jnp.uint32) + zero, F32)


def _ada_kernel(c_ref, w_ref, b_ref, o_ref):
    cond = _silu(c_ref[...])
    o_ref[...] = jnp.dot(cond.astype(BF16), w_ref[...].astype(BF16),
                         preferred_element_type=F32) + b_ref[...]


def _ada(c, ada_w, ada_b):
    depth, d, n = ada_w.shape
    b = c.shape[0]
    tn = n // 4
    return pl.pallas_call(
        _ada_kernel,
        out_shape=jax.ShapeDtypeStruct((depth, b, n), F32),
        grid=(depth, n // tn),
        in_specs=[
            pl.BlockSpec((b, d), lambda l, j: (0, 0)),
            pl.BlockSpec((None, d, tn), lambda l, j: (l, 0, j)),
            pl.BlockSpec((None, 1, tn), lambda l, j: (l, 0, j)),
        ],
        out_specs=pl.BlockSpec((None, b, tn), lambda l, j: (l, 0, j)),
        compiler_params=pltpu.CompilerParams(
            dimension_semantics=("arbitrary", "arbitrary"),
            vmem_limit_bytes=V7X_VMEM_LIMIT_BYTES),
        name="ada_mod",
    )(c, ada_w, ada_b.reshape(depth, 1, n))


def _conv_mlp_kernel(x_ref, modc_ref, modp_ref, gmix_ref, w1_ref, b1_ref, wdw_ref, bdw_ref,
                     lng_ref, lnb_ref, w2_ref, b2_ref, gmlp_ref, m1_hbm, m2_hbm, o_ref,
                     ubuf, cbuf, x1buf, h2buf, tbuf, fbuf, m1_ref, m2_ref, wsem, *, tiles_per_seq, mlp_layer):
    tm, d = x_ref.shape
    groups = d // V7X_LANES
    fw = m1_ref.shape[2]
    ow = m2_ref.shape[2]
    n = pl.program_id(0)

    def weight_block_copies():
        copies = []
        for q in range(MLP_PHASES):
            copies.append(pltpu.make_async_copy(
                m1_hbm.at[mlp_layer, :, pl.ds(q * fw, fw)], m1_ref.at[q], wsem.at[q]))
            copies.append(pltpu.make_async_copy(
                m2_hbm.at[mlp_layer, :, pl.ds(q * ow, ow)], m2_ref.at[q], wsem.at[MLP_PHASES + q]))
        return copies

    @pl.when(n == 0)
    def _():
        for cp in weight_block_copies():
            cp.start()
        x1buf[...] = jnp.zeros(x1buf.shape, F32)
        for cp in weight_block_copies():
            cp.wait()

    @pl.when(n % tiles_per_seq == 0)
    def _():
        ubuf[:, 0:CONV_HALO, :] = jnp.zeros((groups, CONV_HALO, V7X_LANES), F32)

    @pl.when(n % tiles_per_seq != 0)
    def _():
        ubuf[:, 0:CONV_HALO, :] = ubuf[:, tm:tm + CONV_HALO, :]

    h = _rmsnorm_mod(x_ref[...], gmix_ref[...], modc_ref[0:1, :], modc_ref[1:2, :]).astype(BF16)
    for c0 in range(0, d, V7X_MXU_COLS):
        a = jnp.dot(h, w1_ref[:, c0:c0 + V7X_MXU_COLS], preferred_element_type=F32) + b1_ref[:, c0:c0 + V7X_MXU_COLS]
        gt = (jnp.dot(h, w1_ref[:, d + c0:d + c0 + V7X_MXU_COLS], preferred_element_type=F32)
              + b1_ref[:, d + c0:d + c0 + V7X_MXU_COLS])
        u = a * jax.nn.sigmoid(gt)
        for k in range(V7X_MXU_COLS // V7X_LANES):
            ubuf[c0 // V7X_LANES + k, CONV_HALO:CONV_HALO + tm, :] = u[:, k * V7X_LANES:(k + 1) * V7X_LANES]
    h2buf[...] = _rmsnorm_mod(x1buf[...], gmlp_ref[...], modp_ref[3:4, :], modp_ref[4:5, :]).astype(BF16)

    first_tap = CONV_HALO - (CONV_WIDTH - 1)

    def conv_taps(gi, done=None):
        for r0 in range(0, tm, CONV_ROWS):
            win = ubuf.at[gi, r0:r0 + CONV_ROWS + CONV_HALO, :]
            tap = jnp.zeros((CONV_ROWS, V7X_LANES), F32) + bdw_ref[gi]
            if done is not None:
                tap = _after(tap, done)
            for t in range(CONV_WIDTH):
                tap = tap + win[pl.ds(first_tap + t, CONV_ROWS, stride=1), :] * wdw_ref[gi, t:t + 1, :]
            cbuf[gi, r0:r0 + CONV_ROWS, :] = tap
            done = tap
        return done

    def hidden_phase(p, carry):
        for c0 in range(0, fw, V7X_MXU_COLS):
            t1 = jnp.dot(h2buf[...], m1_ref[p, :, c0:c0 + V7X_MXU_COLS], preferred_element_type=F32)
            t1 = jnp.maximum(t1, 0.0)
            tbuf[p, :, c0:c0 + V7X_MXU_COLS] = (t1 * t1).astype(BF16)
        return carry

    lax.fori_loop(0, MLP_PHASES, hidden_phase, 0)

    def out_phase(p, carry):
        ff = jnp.dot(tbuf[0], m2_ref[p, 0:fw, :], preferred_element_type=F32)
        for q in range(1, MLP_PHASES):
            ff = ff + jnp.dot(tbuf[q], m2_ref[p, q * fw:(q + 1) * fw, :], preferred_element_type=F32)
        conv_taps(2 * p + 1, conv_taps(2 * p))
        fbuf[p] = ff
        return carry

    lax.fori_loop(0, MLP_PHASES, out_phase, 0)

    ff = jnp.concatenate([fbuf[p] for p in range(MLP_PHASES)], axis=-1)
    o_ref[...] = x1buf[...] + modp_ref[5:6, :] * ff

    cv = jnp.concatenate([cbuf[gi] for gi in range(groups)], axis=-1)
    mu = jnp.mean(cv, axis=-1, keepdims=True)
    cen = cv - mu
    var = jnp.mean(cen * cen, axis=-1, keepdims=True)
    v = cen * lax.rsqrt(var + EPS) * lng_ref[...] + lnb_ref[...]
    vb = _silu(v).astype(BF16)
    for c0 in range(0, d, V7X_MXU_COLS):
        cols = slice(c0, c0 + V7X_MXU_COLS)
        y = jnp.dot(vb, w2_ref[:, cols], preferred_element_type=F32) + b2_ref[:, cols]
        x1buf[:, cols] = x_ref[:, cols] + modc_ref[2:3, cols] * y


def _conv_mlp_layer(x, mod, gmix, w1, b1, wdw, bdw, lng, lnb, w2, b2, gmlp, m1, m2, conv_layer, mlp_layer):
    b, s, d = x.shape
    d_ff = m1.shape[-1]
    fw = d_ff // MLP_PHASES
    ow = d // MLP_PHASES
    tm = CONV_TM
    tiles_per_seq = s // tm
    n_tiles = b * tiles_per_seq
    groups = d // V7X_LANES
    row = lambda v: v.reshape(1, -1)
    cur = lambda n: jnp.minimum(n, n_tiles - 1)
    prev = lambda n: jnp.maximum(n - 1, 0)
    out = pl.pallas_call(
        functools.partial(_conv_mlp_kernel, tiles_per_seq=tiles_per_seq, mlp_layer=mlp_layer),
        out_shape=jax.ShapeDtypeStruct((b * s, d), F32),
        grid=(n_tiles + 1,),
        in_specs=[
            pl.BlockSpec((tm, d), lambda n: (cur(n), 0)),
            pl.BlockSpec((None, 6, d), lambda n: (cur(n) // tiles_per_seq, 0, 0)),
            pl.BlockSpec((None, 6, d), lambda n: (prev(n) // tiles_per_seq, 0, 0)),
            _resident((1, d)),
            _resident((d, 2 * d), conv_layer),
            _resident((1, 2 * d)),
            _resident((groups, CONV_WIDTH, V7X_LANES)),
            _resident((groups, 1, V7X_LANES)),
            _resident((1, d)),
            _resident((1, d)),
            _resident((d, d), conv_layer),
            _resident((1, d)),
            _resident((1, d)),
            pl.BlockSpec(memory_space=pl.ANY),
            pl.BlockSpec(memory_space=pl.ANY),
        ],
        out_specs=pl.BlockSpec((tm, d), lambda n: (prev(n), 0)),
        scratch_shapes=[
            pltpu.VMEM((groups, CONV_HALO + tm, V7X_LANES), F32),
            pltpu.VMEM((groups, tm, V7X_LANES), F32),
            pltpu.VMEM((tm, d), F32),
            pltpu.VMEM((tm, d), BF16),
            pltpu.VMEM((MLP_PHASES, tm, fw), BF16),
            pltpu.VMEM((MLP_PHASES, tm, ow), F32),
            pltpu.VMEM((MLP_PHASES, d, fw), BF16),
            pltpu.VMEM((MLP_PHASES, d_ff, ow), BF16),
            pltpu.SemaphoreType.DMA((2 * MLP_PHASES,)),
        ],
        compiler_params=pltpu.CompilerParams(
            dimension_semantics=("arbitrary",),
            vmem_limit_bytes=V7X_VMEM_LIMIT_BYTES),
        name="conv_mlp",
    )(x.reshape(b * s, d), mod, mod, row(gmix), w1, row(b1),
      wdw.reshape(CONV_WIDTH, groups, V7X_LANES).transpose(1, 0, 2), bdw.reshape(groups, 1, V7X_LANES),
      row(lng), row(lnb), w2, row(b2), row(gmlp), m1, m2)
    return out.reshape(b, s, d)


def _mlp_kernel(x_ref, mod_ref, g_ref, w1_ref, w2_ref, fg_ref, o_ref, *, final):
    d_ff = w1_ref.shape[1]
    fw = d_ff // MLP_PHASES
    x = x_ref[...]
    h = _rmsnorm_mod(x, g_ref[...], mod_ref[3:4, :], mod_ref[4:5, :]).astype(BF16)
    acc = jnp.zeros(x.shape, F32)
    for c0 in range(0, d_ff, fw):
        t = jnp.maximum(jnp.dot(h, w1_ref[:, c0:c0 + fw], preferred_element_type=F32), 0.0)
        acc = acc + jnp.dot((t * t).astype(BF16), w2_ref[c0:c0 + fw, :], preferred_element_type=F32)
    y = x + mod_ref[5:6, :] * acc
    if final:
        y = y * lax.rsqrt(jnp.mean(y * y, axis=-1, keepdims=True) + EPS) * fg_ref[...]
    o_ref[...] = y


def _mlp_layer(x, mod, g, w1, w2, layer, fg, final):
    b, s, d = x.shape
    d_ff = w1.shape[-1]
    tm = MLP_TM
    return pl.pallas_call(
        functools.partial(_mlp_kernel, final=final),
        out_shape=jax.ShapeDtypeStruct(x.shape, F32),
        grid=(b, s // tm),
        in_specs=[
            pl.BlockSpec((None, tm, d), lambda bi, j: (bi, j, 0)),
            pl.BlockSpec((None, 6, d), lambda bi, j: (bi, 0, 0)),
            _resident((1, d)),
            _resident((d, d_ff), layer),
            _resident((d_ff, d), layer),
            _resident((1, d)),
        ],
        out_specs=pl.BlockSpec((None, tm, d), lambda bi, j: (bi, j, 0)),
        compiler_params=pltpu.CompilerParams(
            dimension_semantics=("arbitrary", "arbitrary"),
            vmem_limit_bytes=V7X_VMEM_LIMIT_BYTES),
        name="mlp_final" if final else "mlp",
    )(x, mod, g.reshape(1, d), w1, w2, fg.reshape(1, d))


def _log_gamma(head):
    return float(np.log(np.float32(1.0) - np.float32(2.0) ** np.float32(-5.0 - head)))


def _ret_kernel(x_ref, xnext_ref, mod_ref, modnext_ref, g_ref, cos_ref, sin_ref, win_ref, gng_ref, gnb_ref,
                wout_ref, o_ref, hbuf, hnext, proj, ybuf, state, dmask, *, tiles_per_seq):
    L, d = x_ref.shape
    heads = RET_HEADS
    dk = d // heads
    dv = 2 * d // heads
    half = dk // 2
    n = pl.program_id(0)

    @pl.when(n == 0)
    def _():
        hnext[...] = _rmsnorm_mod(x_ref[...], g_ref[...], mod_ref[0:1, :], mod_ref[1:2, :]).astype(BF16)
        r = lax.broadcasted_iota(jnp.int32, (L, L), 0)
        c = lax.broadcasted_iota(jnp.int32, (L, L), 1)
        dist = jnp.abs(r - c).astype(F32)
        chunk_shift = CHUNK.bit_length() - 1
        visible = jnp.right_shift(c, chunk_shift) <= jnp.right_shift(r, chunk_shift)
        for hd in range(heads):
            dmask[hd] = jnp.where(visible, jnp.exp(_log_gamma(hd) * dist), 0.0)

    @pl.when(n % tiles_per_seq == 0)
    def _():
        state[...] = jnp.zeros(state.shape, F32)

    hbuf[...] = hnext[...]
    x = x_ref[...]
    h = hbuf[...]
    n_in = win_ref.shape[1]
    for c0 in range(0, n_in, d):
        proj[:, c0:c0 + d] = jnp.dot(h, win_ref[:, c0:c0 + d], preferred_element_type=F32)

    hn = _rmsnorm_mod(xnext_ref[...], g_ref[...], modnext_ref[0:1, :], modnext_ref[1:2, :]).astype(BF16)
    hnext[...] = hn

    cos = cos_ref[...]
    sin = sin_ref[...]
    last = hn[L - 2 * V7X_SUBLANES:L, d - V7X_LANES:d].astype(F32)[0:V7X_SUBLANES, :]
    cos = jnp.concatenate([_after(cos[0:V7X_SUBLANES, :], last), cos[V7X_SUBLANES:, :]], axis=0)
    idx = lax.broadcasted_iota(jnp.int32, (L, 1), 0).astype(F32)
    k_off, v_off, g_off = d, 2 * d, 2 * d + heads * dv

    def rope(base):
        x1 = proj[:, base:base + half]
        x2 = proj[:, base + half:base + dk]
        return jnp.concatenate([x1 * cos - x2 * sin, x2 * cos + x1 * sin], axis=-1)

    for hd in range(heads):
        lg = _log_gamma(hd)
        q = rope(hd * dk)
        k = rope(k_off + hd * dk) * (dk ** -0.5)
        vb = proj[:, v_off + hd * dv:v_off + (hd + 1) * dv].astype(BF16)
        qb = q.astype(BF16)
        scores = lax.dot_general(qb, k.astype(BF16), (((1,), (1,)), ((), ())),
                                 preferred_element_type=F32) * dmask[hd]
        intra = jnp.dot(scores.astype(BF16), vb, preferred_element_type=F32)
        xi = jnp.exp(lg * (idx + 1.0))
        st = state[hd]
        cross = jnp.dot(qb, st.astype(BF16), preferred_element_type=F32) * xi
        zeta = jnp.exp(lg * (float(L - 1) - idx))
        kz = (k * zeta).astype(BF16)
        state[hd] = st * float(np.exp(np.float32(lg) * np.float32(L))) + lax.dot_general(
            kz, vb, (((0,), (0,)), ((), ())), preferred_element_type=F32)
        y = intra + cross
        mu = jnp.mean(y, axis=-1, keepdims=True)
        cen = y - mu
        var = jnp.mean(cen * cen, axis=-1, keepdims=True)
        yn = cen * lax.rsqrt(var + EPS) * gng_ref[hd:hd + 1, :] + gnb_ref[hd:hd + 1, :]
        gate = proj[:, g_off + hd * dv:g_off + (hd + 1) * dv]
        ybuf[:, hd * dv:(hd + 1) * dv] = (_silu(gate) * yn).astype(BF16)

    out = jnp.dot(ybuf[...], wout_ref[...], preferred_element_type=F32)
    o_ref[...] = x + mod_ref[2:3, :] * out


def _ret_layer(x, mod, g, cos, sin, w_in, gn_g, gn_b, w_out, ret_layer):
    b, s, d = x.shape
    L = RET_L
    heads = RET_HEADS
    dk, dv = d // heads, 2 * d // heads
    n_in = w_in.shape[-1]
    tiles_per_seq = s // L
    n_tiles = b * tiles_per_seq
    nxt = lambda n: jnp.minimum(n + 1, n_tiles - 1)
    x2 = x.reshape(b * s, d)
    out = pl.pallas_call(
        functools.partial(_ret_kernel, tiles_per_seq=tiles_per_seq),
        out_shape=jax.ShapeDtypeStruct((b * s, d), F32),
        grid=(n_tiles,),
        in_specs=[
            pl.BlockSpec((L, d), lambda n: (n, 0)),
            pl.BlockSpec((L, d), lambda n: (nxt(n), 0)),
            pl.BlockSpec((None, 6, d), lambda n: (n // tiles_per_seq, 0, 0)),
            pl.BlockSpec((None, 6, d), lambda n: (nxt(n) // tiles_per_seq, 0, 0)),
            _resident((1, d)),
            pl.BlockSpec((L, dk // 2), lambda n: (n % tiles_per_seq, 0)),
            pl.BlockSpec((L, dk // 2), lambda n: (n % tiles_per_seq, 0)),
            _resident((d, n_in), ret_layer),
            _resident((heads, dv)),
            _resident((heads, dv)),
            _resident((heads * dv, d), ret_layer),
        ],
        out_specs=pl.BlockSpec((L, d), lambda n: (n, 0)),
        scratch_shapes=[
            pltpu.VMEM((L, d), BF16),
            pltpu.VMEM((L, d), BF16),
            pltpu.VMEM((L, n_in), F32),
            pltpu.VMEM((L, heads * dv), BF16),
            pltpu.VMEM((heads, dk, dv), F32),
            pltpu.VMEM((heads, L, L), F32),
        ],
        compiler_params=pltpu.CompilerParams(
            dimension_semantics=("arbitrary",),
            vmem_limit_bytes=V7X_VMEM_LIMIT_BYTES),
        name="retention_mixer",
    )(x2, x2, mod, mod, g.reshape(1, d), cos, sin, w_in, gn_g, gn_b, w_out)
    return out.reshape(b, s, d)


def _rope_tables(seq, dk):
    pos = jnp.arange(seq, dtype=F32)
    inv = ROPE_BASE ** (-jnp.arange(0, dk, 2, dtype=F32) / dk)
    ang = pos[:, None] * inv[None, :]
    return jnp.cos(ang), jnp.sin(ang)


def kernel(x, c, ada_w, ada_b, norm_mix_g, norm_mlp_g, conv_w_pw1, conv_b_pw1, conv_w_dw, conv_b_dw, conv_ln_g, conv_ln_b, conv_w_pw2, conv_b_pw2, ret_w_in, ret_gn_g, ret_gn_b, ret_w_out, mlp_w1, mlp_w2, final_norm_g):
    depth = ada_w.shape[0]
    b, s, d = x.shape
    assert s % CONV_TM == 0 and s % MLP_TM == 0 and s % RET_L == 0 and RET_L % CHUNK == 0
    assert CONV_HALO >= CONV_WIDTH - 1 and CONV_HALO % V7X_SUBLANES == 0 and CONV_TM % CONV_ROWS == 0
    assert d == 2 * MLP_PHASES * V7X_LANES

    mod = _ada(c, ada_w, ada_b).reshape(depth, b, 6, d)
    cos, sin = _rope_tables(s, d // RET_HEADS)
    pw1, pw2 = conv_w_pw1.astype(BF16), conv_w_pw2.astype(BF16)
    w_in, w_out = ret_w_in.astype(BF16), ret_w_out.astype(BF16)
    m1, m2 = mlp_w1.astype(BF16), mlp_w2.astype(BF16)
    for i in range(depth):
        jm = i // 2
        if i % 2 == 0:
            assert i != depth - 1
            x = _conv_mlp_layer(x, mod[i], norm_mix_g[i], pw1, conv_b_pw1[jm], conv_w_dw[jm], conv_b_dw[jm],
                                conv_ln_g[jm], conv_ln_b[jm], pw2, conv_b_pw2[jm], norm_mlp_g[i],
                                m1, m2, conv_layer=jm, mlp_layer=i)
        else:
            x = _ret_layer(x, mod[i], norm_mix_g[i], cos, sin, w_in, ret_gn_g[jm], ret_gn_b[jm], w_out, ret_layer=jm)
            x = _mlp_layer(x, mod[i], norm_mlp_g[i], m1, m2, i, final_norm_g, final=(i == depth - 1))
    return x
```

```python
import functools

import numpy as np
import jax
import jax.numpy as jnp
from jax import lax
from jax.experimental import pallas as pl
from jax.experimental.pallas import tpu as pltpu

F32 = jnp.float32
BF16 = jnp.bfloat16

EPS = 1e-6
CHUNK = 64
CONV_WIDTH = 31
RET_HEADS = 4
ROPE_BASE = 10000.0

V7X_SUBLANES = 8
V7X_LANES = 128
V7X_MXU_COLS = 256
V7X_VMEM_LIMIT_BYTES = 56 * 1024 * 1024

CONV_TM = 512
CONV_HALO = 32
CONV_ROWS = 16
MLP_PHASES = 4
MLP_TM = 1024
RET_L = 256


def _resident(shape, layer=None):
    zeros = (0,) * len(shape)
    if layer is None:
        return pl.BlockSpec(shape, lambda *_: zeros, pipeline_mode=pl.Buffered(1))
    return pl.BlockSpec((None,) + tuple(shape), lambda *_: (layer,) + zeros, pipeline_mode=pl.Buffered(1))


def _rmsnorm_mod(x, g, shift, scale):
    y = x * lax.rsqrt(jnp.mean(x * x, axis=-1, keepdims=True) + EPS)
    return (y * g) * (1.0 + scale) + shift


def _silu(v):
    return v * jax.nn.sigmoid(v)


def _after(value, dep):
    bits = pltpu.bitcast(dep, jnp.uint32)
    zero = lax.shift_right_logical(lax.shift_right_logical(bits, jnp.uint32(16)), jnp.uint32(16))
    return pltpu.bitcast(pltpu.bitcast(value, jnp.uint32) + zero, F32)


def _ada_kernel(c_ref, w_ref, b_ref, o_ref):
    cond = _silu(c_ref[...])
    o_ref[...] = jnp.dot(cond.astype(BF16), w_ref[...].astype(BF16),
                         preferred_element_type=F32) + b_ref[...]


def _ada(c, ada_w, ada_b):
    depth, d, n = ada_w.shape
    b = c.shape[0]
    tn = n // 4
    return pl.pallas_call(
        _ada_kernel,
        out_shape=jax.ShapeDtypeStruct((depth, b, n), F32),
        grid=(depth, n // tn),
        in_specs=[
            pl.BlockSpec((b, d), lambda l, j: (0, 0)),
            pl.BlockSpec((None, d, tn), lambda l, j: (l, 0, j)),
            pl.BlockSpec((None, 1, tn), lambda l, j: (l, 0, j)),
        ],
        out_specs=pl.BlockSpec((None, b, tn), lambda l, j: (l, 0, j)),
        compiler_params=pltpu.CompilerParams(
            dimension_semantics=("arbitrary", "arbitrary"),
            vmem_limit_bytes=V7X_VMEM_LIMIT_BYTES),
        name="ada_mod",
    )(c, ada_w, ada_b.reshape(depth, 1, n))


def _conv_mlp_kernel(x_ref, modc_ref, modp_ref, gmix_ref, w1_ref, b1_ref, wdw_ref, bdw_ref,
                     lng_ref, lnb_ref, w2_ref, b2_ref, gmlp_ref, m1_hbm, m2_hbm, o_ref,
                     ubuf, cbuf, x1buf, h2buf, tbuf, fbuf, m1_ref, m2_ref, wsem, *, tiles_per_seq, mlp_layer):
    tm, d = x_ref.shape
    groups = d // V7X_LANES
    fw = m1_ref.shape[2]
    ow = m2_ref.shape[2]
    n = pl.program_id(0)

    def weight_block_copies():
        copies = []
        for q in range(MLP_PHASES):
            copies.append(pltpu.make_async_copy(
                m1_hbm.at[mlp_layer, :, pl.ds(q * fw, fw)], m1_ref.at[q], wsem.at[q]))
            copies.append(pltpu.make_async_copy(
                m2_hbm.at[mlp_layer, :, pl.ds(q * ow, ow)], m2_ref.at[q], wsem.at[MLP_PHASES + q]))
        return copies

    @pl.when(n == 0)
    def _():
        for cp in weight_block_copies():
            cp.start()
        x1buf[...] = jnp.zeros(x1buf.shape, F32)
        for cp in weight_block_copies():
            cp.wait()

    @pl.when(n % tiles_per_seq == 0)
    def _():
        ubuf[:, 0:CONV_HALO, :] = jnp.zeros((groups, CONV_HALO, V7X_LANES), F32)

    @pl.when(n % tiles_per_seq != 0)
    def _():
        ubuf[:, 0:CONV_HALO, :] = ubuf[:, tm:tm + CONV_HALO, :]

    h = _rmsnorm_mod(x_ref[...], gmix_ref[...], modc_ref[0:1, :], modc_ref[1:2, :]).astype(BF16)
    for c0 in range(0, d, V7X_MXU_COLS):
        a = jnp.dot(h, w1_ref[:, c0:c0 + V7X_MXU_COLS], preferred_element_type=F32) + b1_ref[:, c0:c0 + V7X_MXU_COLS]
        gt = (jnp.dot(h, w1_ref[:, d + c0:d + c0 + V7X_MXU_COLS], preferred_element_type=F32)
              + b1_ref[:, d + c0:d + c0 + V7X_MXU_COLS])
        u = a * jax.nn.sigmoid(gt)
        for k in range(V7X_MXU_COLS // V7X_LANES):
            ubuf[c0 // V7X_LANES + k, CONV_HALO:CONV_HALO + tm, :] = u[:, k * V7X_LANES:(k + 1) * V7X_LANES]
    h2buf[...] = _rmsnorm_mod(x1buf[...], gmlp_ref[...], modp_ref[3:4, :], modp_ref[4:5, :]).astype(BF16)

    first_tap = CONV_HALO - (CONV_WIDTH - 1)

    def conv_taps(gi):
        done = None
        for r0 in range(0, tm, CONV_ROWS):
            win = ubuf.at[gi, r0:r0 + CONV_ROWS + CONV_HALO, :]
            tap = jnp.zeros((CONV_ROWS, V7X_LANES), F32) + bdw_ref[gi]
            if done is not None:
                tap = _after(tap, done)
            for t in range(CONV_WIDTH):
                tap = tap + win[pl.ds(first_tap + t, CONV_ROWS, stride=1), :] * wdw_ref[gi, t:t + 1, :]
            cbuf[gi, r0:r0 + CONV_ROWS, :] = tap
            done = tap

    def hidden_phase(p, carry):
        for c0 in range(0, fw, V7X_MXU_COLS):
            t1 = jnp.dot(h2buf[...], m1_ref[p, :, c0:c0 + V7X_MXU_COLS], preferred_element_type=F32)
            t1 = jnp.maximum(t1, 0.0)
            tbuf[p, :, c0:c0 + V7X_MXU_COLS] = (t1 * t1).astype(BF16)
        conv_taps(p)
        return carry

    lax.fori_loop(0, MLP_PHASES, hidden_phase, 0)

    def out_phase(p, carry):
        ff = jnp.dot(tbuf[0], m2_ref[p, 0:fw, :], preferred_element_type=F32)
        for q in range(1, MLP_PHASES):
            ff = ff + jnp.dot(tbuf[q], m2_ref[p, q * fw:(q + 1) * fw, :], preferred_element_type=F32)
        conv_taps(MLP_PHASES + p)
        fbuf[p] = ff
        return carry

    lax.fori_loop(0, MLP_PHASES, out_phase, 0)

    ff = jnp.concatenate([fbuf[p] for p in range(MLP_PHASES)], axis=-1)
    o_ref[...] = x1buf[...] + modp_ref[5:6, :] * ff

    cv = jnp.concatenate([cbuf[gi] for gi in range(groups)], axis=-1)
    mu = jnp.mean(cv, axis=-1, keepdims=True)
    cen = cv - mu
    var = jnp.mean(cen * cen, axis=-1, keepdims=True)
    v = cen * lax.rsqrt(var + EPS) * lng_ref[...] + lnb_ref[...]
    vb = _silu(v).astype(BF16)
    for c0 in range(0, d, V7X_MXU_COLS):
        cols = slice(c0, c0 + V7X_MXU_COLS)
        y = jnp.dot(vb, w2_ref[:, cols], preferred_element_type=F32) + b2_ref[:, cols]
        x1buf[:, cols] = x_ref[:, cols] + modc_ref[2:3, cols] * y


def _conv_mlp_layer(x, mod, gmix, w1, b1, wdw, bdw, lng, lnb, w2, b2, gmlp, m1, m2, conv_layer, mlp_layer):
    b, s, d = x.shape
    d_ff = m1.shape[-1]
    fw = d_ff // MLP_PHASES
    ow = d // MLP_PHASES
    tm = CONV_TM
    tiles_per_seq = s // tm
    n_tiles = b * tiles_per_seq
    groups = d // V7X_LANES
    row = lambda v: v.reshape(1, -1)
    cur = lambda n: jnp.minimum(n, n_tiles - 1)
    prev = lambda n: jnp.maximum(n - 1, 0)
    out = pl.pallas_call(
        functools.partial(_conv_mlp_kernel, tiles_per_seq=tiles_per_seq, mlp_layer=mlp_layer),
        out_shape=jax.ShapeDtypeStruct((b * s, d), F32),
        grid=(n_tiles + 1,),
        in_specs=[
            pl.BlockSpec((tm, d), lambda n: (cur(n), 0)),
            pl.BlockSpec((None, 6, d), lambda n: (cur(n) // tiles_per_seq, 0, 0)),
            pl.BlockSpec((None, 6, d), lambda n: (prev(n) // tiles_per_seq, 0, 0)),
            _resident((1, d)),
            _resident((d, 2 * d), conv_layer),
            _resident((1, 2 * d)),
            _resident((groups, CONV_WIDTH, V7X_LANES)),
            _resident((groups, 1, V7X_LANES)),
            _resident((1, d)),
            _resident((1, d)),
            _resident((d, d), conv_layer),
            _resident((1, d)),
            _resident((1, d)),
            pl.BlockSpec(memory_space=pl.ANY),
            pl.BlockSpec(memory_space=pl.ANY),
        ],
        out_specs=pl.BlockSpec((tm, d), lambda n: (prev(n), 0)),
        scratch_shapes=[
            pltpu.VMEM((groups, CONV_HALO + tm, V7X_LANES), F32),
            pltpu.VMEM((groups, tm, V7X_LANES), F32),
            pltpu.VMEM((tm, d), F32),
            pltpu.VMEM((tm, d), BF16),
            pltpu.VMEM((MLP_PHASES, tm, fw), BF16),
            pltpu.VMEM((MLP_PHASES, tm, ow), F32),
            pltpu.VMEM((MLP_PHASES, d, fw), BF16),
            pltpu.VMEM((MLP_PHASES, d_ff, ow), BF16),
            pltpu.SemaphoreType.DMA((2 * MLP_PHASES,)),
        ],
        compiler_params=pltpu.CompilerParams(
            dimension_semantics=("arbitrary",),
            vmem_limit_bytes=V7X_VMEM_LIMIT_BYTES),
        name="conv_mlp",
    )(x.reshape(b * s, d), mod, mod, row(gmix), w1, row(b1),
      wdw.reshape(CONV_WIDTH, groups, V7X_LANES).transpose(1, 0, 2), bdw.reshape(groups, 1, V7X_LANES),
      row(lng), row(lnb), w2, row(b2), row(gmlp), m1, m2)
    return out.reshape(b, s, d)


def _mlp_kernel(x_ref, mod_ref, g_ref, w1_ref, w2_ref, fg_ref, o_ref, *, final):
    d_ff = w1_ref.shape[1]
    fw = d_ff // MLP_PHASES
    x = x_ref[...]
    h = _rmsnorm_mod(x, g_ref[...], mod_ref[3:4, :], mod_ref[4:5, :]).astype(BF16)
    acc = jnp.zeros(x.shape, F32)
    for c0 in range(0, d_ff, fw):
        t = jnp.maximum(jnp.dot(h, w1_ref[:, c0:c0 + fw], preferred_element_type=F32), 0.0)
        acc = acc + jnp.dot((t * t).astype(BF16), w2_ref[c0:c0 + fw, :], preferred_element_type=F32)
    y = x + mod_ref[5:6, :] * acc
    if final:
        y = y * lax.rsqrt(jnp.mean(y * y, axis=-1, keepdims=True) + EPS) * fg_ref[...]
    o_ref[...] = y


def _mlp_layer(x, mod, g, w1, w2, layer, fg, final):
    b, s, d = x.shape
    d_ff = w1.shape[-1]
    tm = MLP_TM
    return pl.pallas_call(
        functools.partial(_mlp_kernel, final=final),
        out_shape=jax.ShapeDtypeStruct(x.shape, F32),
        grid=(b, s // tm),
        in_specs=[
            pl.BlockSpec((None, tm, d), lambda bi, j: (bi, j, 0)),
            pl.BlockSpec((None, 6, d), lambda bi, j: (bi, 0, 0)),
            _resident((1, d)),
            _resident((d, d_ff), layer),
            _resident((d_ff, d), layer),
            _resident((1, d)),
        ],
        out_specs=pl.BlockSpec((None, tm, d), lambda bi, j: (bi, j, 0)),
        compiler_params=pltpu.CompilerParams(
            dimension_semantics=("arbitrary", "arbitrary"),
            vmem_limit_bytes=V7X_VMEM_LIMIT_BYTES),
        name="mlp_final" if final else "mlp",
    )(x, mod, g.reshape(1, d), w1, w2, fg.reshape(1, d))


def _log_gamma(head):
    return float(np.log(np.float32(1.0) - np.float32(2.0) ** np.float32(-5.0 - head)))


def _ret_kernel(x_ref, xnext_ref, mod_ref, modnext_ref, g_ref, cos_ref, sin_ref, win_ref, gng_ref, gnb_ref,
                wout_ref, o_ref, hbuf, hnext, proj, ybuf, state, dmask, *, tiles_per_seq):
    L, d = x_ref.shape
    heads = RET_HEADS
    dk = d // heads
    dv = 2 * d // heads
    half = dk // 2
    n = pl.program_id(0)

    @pl.when(n == 0)
    def _():
        hnext[...] = _rmsnorm_mod(x_ref[...], g_ref[...], mod_ref[0:1, :], mod_ref[1:2, :]).astype(BF16)
        r = lax.broadcasted_iota(jnp.int32, (L, L), 0)
        c = lax.broadcasted_iota(jnp.int32, (L, L), 1)
        dist = jnp.abs(r - c).astype(F32)
        chunk_shift = CHUNK.bit_length() - 1
        visible = jnp.right_shift(c, chunk_shift) <= jnp.right_shift(r, chunk_shift)
        for hd in range(heads):
            dmask[hd] = jnp.where(visible, jnp.exp(_log_gamma(hd) * dist), 0.0)

    @pl.when(n % tiles_per_seq == 0)
    def _():
        state[...] = jnp.zeros(state.shape, F32)

    hbuf[...] = hnext[...]
    x = x_ref[...]
    h = hbuf[...]
    n_in = win_ref.shape[1]
    for c0 in range(0, n_in, d):
        proj[:, c0:c0 + d] = jnp.dot(h, win_ref[:, c0:c0 + d], preferred_element_type=F32)

    hn = _rmsnorm_mod(xnext_ref[...], g_ref[...], modnext_ref[0:1, :], modnext_ref[1:2, :]).astype(BF16)
    hnext[...] = hn

    cos = cos_ref[...]
    sin = sin_ref[...]
    last = hn[L - 2 * V7X_SUBLANES:L, d - V7X_LANES:d].astype(F32)[0:V7X_SUBLANES, :]
    cos = jnp.concatenate([_after(cos[0:V7X_SUBLANES, :], last), cos[V7X_SUBLANES:, :]], axis=0)
    idx = lax.broadcasted_iota(jnp.int32, (L, 1), 0).astype(F32)
    k_off, v_off, g_off = d, 2 * d, 2 * d + heads * dv

    def rope(base):
        x1 = proj[:, base:base + half]
        x2 = proj[:, base + half:base + dk]
        return jnp.concatenate([x1 * cos - x2 * sin, x2 * cos + x1 * sin], axis=-1)

    for hd in range(heads):
        lg = _log_gamma(hd)
        q = rope(hd * dk)
        k = rope(k_off + hd * dk) * (dk ** -0.5)
        vb = proj[:, v_off + hd * dv:v_off + (hd + 1) * dv].astype(BF16)
        qb = q.astype(BF16)
        scores = lax.dot_general(qb, k.astype(BF16), (((1,), (1,)), ((), ())),
                                 preferred_element_type=F32) * dmask[hd]
        intra = jnp.dot(scores.astype(BF16), vb, preferred_element_type=F32)
        xi = jnp.exp(lg * (idx + 1.0))
        st = state[hd]
        cross = jnp.dot(qb, st.astype(BF16), preferred_element_type=F32) * xi
        zeta = jnp.exp(lg * (float(L - 1) - idx))
        kz = (k * zeta).astype(BF16)
        state[hd] = st * float(np.exp(np.float32(lg) * np.float32(L))) + lax.dot_general(
            kz, vb, (((0,), (0,)), ((), ())), preferred_element_type=F32)
        y = intra + cross
        mu = jnp.mean(y, axis=-1, keepdims=True)
        cen = y - mu
        var = jnp.mean(cen * cen, axis=-1, keepdims=True)
        yn = cen * lax.rsqrt(var + EPS) * gng_ref[hd:hd + 1, :] + gnb_ref[hd:hd + 1, :]
        gate = proj[:, g_off + hd * dv:g_off + (hd + 1) * dv]
        ybuf[:, hd * dv:(hd + 1) * dv] = (_silu(gate) * yn).astype(BF16)

    out = jnp.dot(ybuf[...], wout_ref[...], preferred_element_type=F32)
    o_ref[...] = x + mod_ref[2:3, :] * out


def _ret_layer(x, mod, g, cos, sin, w_in, gn_g, gn_b, w_out, ret_layer):
    b, s, d = x.shape
    L = RET_L
    heads = RET_HEADS
    dk, dv = d // heads, 2 * d // heads
    n_in = w_in.shape[-1]
    tiles_per_seq = s // L
    n_tiles = b * tiles_per_seq
    nxt = lambda n: jnp.minimum(n + 1, n_tiles - 1)
    x2 = x.reshape(b * s, d)
    out = pl.pallas_call(
        functools.partial(_ret_kernel, tiles_per_seq=tiles_per_seq),
        out_shape=jax.ShapeDtypeStruct((b * s, d), F32),
        grid=(n_tiles,),
        in_specs=[
            pl.BlockSpec((L, d), lambda n: (n, 0)),
            pl.BlockSpec((L, d), lambda n: (nxt(n), 0)),
            pl.BlockSpec((None, 6, d), lambda n: (n // tiles_per_seq, 0, 0)),
            pl.BlockSpec((None, 6, d), lambda n: (nxt(n) // tiles_per_seq, 0, 0)),
            _resident((1, d)),
            pl.BlockSpec((L, dk // 2), lambda n: (n % tiles_per_seq, 0)),
            pl.BlockSpec((L, dk // 2), lambda n: (n % tiles_per_seq, 0)),
            _resident((d, n_in), ret_layer),
            _resident((heads, dv)),
            _resident((heads, dv)),
            _resident((heads * dv, d), ret_layer),
        ],
        out_specs=pl.BlockSpec((L, d), lambda n: (n, 0)),
        scratch_shapes=[
            pltpu.VMEM((L, d), BF16),
            pltpu.VMEM((L, d), BF16),
            pltpu.VMEM((L, n_in), F32),
            pltpu.VMEM((L, heads * dv), BF16),
            pltpu.VMEM((heads, dk, dv), F32),
            pltpu.VMEM((heads, L, L), F32),
        ],
        compiler_params=pltpu.CompilerParams(
            dimension_semantics=("arbitrary",),
            vmem_limit_bytes=V7X_VMEM_LIMIT_BYTES),
        name="retention_mixer",
    )(x2, x2, mod, mod, g.reshape(1, d), cos, sin, w_in, gn_g, gn_b, w_out)
    return out.reshape(b, s, d)


def _rope_tables(seq, dk):
    pos = jnp.arange(seq, dtype=F32)
    inv = ROPE_BASE ** (-jnp.arange(0, dk, 2, dtype=F32) / dk)
    ang = pos[:, None] * inv[None, :]
    return jnp.cos(ang), jnp.sin(ang)


def kernel(x, c, ada_w, ada_b, norm_mix_g, norm_mlp_g, conv_w_pw1, conv_b_pw1, conv_w_dw, conv_b_dw, conv_ln_g, conv_ln_b, conv_w_pw2, conv_b_pw2, ret_w_in, ret_gn_g, ret_gn_b, ret_w_out, mlp_w1, mlp_w2, final_norm_g):
    depth = ada_w.shape[0]
    b, s, d = x.shape
    assert s % CONV_TM == 0 and s % MLP_TM == 0 and s % RET_L == 0 and RET_L % CHUNK == 0
    assert CONV_HALO >= CONV_WIDTH - 1 and CONV_HALO % V7X_SUBLANES == 0 and CONV_TM % CONV_ROWS == 0
    assert d == 2 * MLP_PHASES * V7X_LANES

    mod = _ada(c, ada_w, ada_b).reshape(depth, b, 6, d)
    cos, sin = _rope_tables(s, d // RET_HEADS)
    pw1, pw2 = conv_w_pw1.astype(BF16), conv_w_pw2.astype(BF16)
    w_in, w_out = ret_w_in.astype(BF16), ret_w_out.astype(BF16)
    m1, m2 = mlp_w1.astype(BF16), mlp_w2.astype(BF16)
    for i in range(depth):
        jm = i // 2
        if i % 2 == 0:
            assert i != depth - 1
            x = _conv_mlp_layer(x, mod[i], norm_mix_g[i], pw1, conv_b_pw1[jm], conv_w_dw[jm], conv_b_dw[jm],
                                conv_ln_g[jm], conv_ln_b[jm], pw2, conv_b_pw2[jm], norm_mlp_g[i],
                                m1, m2, conv_layer=jm, mlp_layer=i)
        else:
            x = _ret_layer(x, mod[i], norm_mix_g[i], cos, sin, w_in, ret_gn_g[jm], ret_gn_b[jm], w_out, ret_layer=jm)
            x = _mlp_layer(x, mod[i], norm_mlp_g[i], m1, m2, i, final_norm_g, final=(i == depth - 1))
    return x
```

```python
import functools

import numpy as np
import jax
import jax.numpy as jnp
from jax import lax
from jax.experimental import pallas as pl
from jax.experimental.pallas import tpu as pltpu

F32 = jnp.float32
BF16 = jnp.bfloat16

EPS = 1e-6
CHUNK = 64
CONV_WIDTH = 31
RET_HEADS = 4
ROPE_BASE = 10000.0

V7X_SUBLANES = 8
V7X_LANES = 128
V7X_MXU_COLS = 256
V7X_VMEM_LIMIT_BYTES = 56 * 1024 * 1024

CONV_TM = 512
CONV_HALO = 32
CONV_ROWS = 16
MLP_PHASES = 4
MLP_TM = 1024
RET_L = 256
RET_TM = 512


def _resident(shape, layer=None):
    zeros = (0,) * len(shape)
    if layer is None:
        return pl.BlockSpec(shape, lambda *_: zeros, pipeline_mode=pl.Buffered(1))
    return pl.BlockSpec((None,) + tuple(shape), lambda *_: (layer,) + zeros, pipeline_mode=pl.Buffered(1))


def _rmsnorm_mod(x, g, shift, scale):
    y = x * lax.rsqrt(jnp.mean(x * x, axis=-1, keepdims=True) + EPS)
    return (y * g) * (1.0 + scale) + shift


def _silu(v):
    return v * jax.nn.sigmoid(v)


def _after(value, dep):
    bits = pltpu.bitcast(dep, jnp.uint32)
    zero = lax.shift_right_logical(lax.shift_right_logical(bits, jnp.uint32(16)), jnp.uint32(16))
    return pltpu.bitcast(pltpu.bitcast(value, jnp.uint32) + zero, F32)


def _ada_kernel(c_ref, w_ref, b_ref, o_ref):
    cond = _silu(c_ref[...])
    o_ref[...] = jnp.dot(cond.astype(BF16), w_ref[...].astype(BF16),
                         preferred_element_type=F32) + b_ref[...]


def _ada(c, ada_w, ada_b):
    depth, d, n = ada_w.shape
    b = c.shape[0]
    tn = n // 4
    return pl.pallas_call(
        _ada_kernel,
        out_shape=jax.ShapeDtypeStruct((depth, b, n), F32),
        grid=(depth, n // tn),
        in_specs=[
            pl.BlockSpec((b, d), lambda l, j: (0, 0)),
            pl.BlockSpec((None, d, tn), lambda l, j: (l, 0, j)),
            pl.BlockSpec((None, 1, tn), lambda l, j: (l, 0, j)),
        ],
        out_specs=pl.BlockSpec((None, b, tn), lambda l, j: (l, 0, j)),
        compiler_params=pltpu.CompilerParams(
            dimension_semantics=("arbitrary", "arbitrary"),
            vmem_limit_bytes=V7X_VMEM_LIMIT_BYTES),
        name="ada_mod",
    )(c, ada_w, ada_b.reshape(depth, 1, n))


def _conv_mlp_kernel(x_ref, modc_ref, modp_ref, gmix_ref, w1_ref, b1_ref, wdw_ref, bdw_ref,
                     lng_ref, lnb_ref, w2_ref, b2_ref, gmlp_ref, m1_hbm, m2_hbm, o_ref,
                     ubuf, cbuf, x1buf, h2buf, tbuf, fbuf, m1_ref, m2_ref, wsem, *, tiles_per_seq, mlp_layer):
    tm, d = x_ref.shape
    groups = d // V7X_LANES
    fw = m1_ref.shape[2]
    ow = m2_ref.shape[2]
    n = pl.program_id(0)

    def weight_block_copies():
        copies = []
        for q in range(MLP_PHASES):
            copies.append(pltpu.make_async_copy(
                m1_hbm.at[mlp_layer, :, pl.ds(q * fw, fw)], m1_ref.at[q], wsem.at[q]))
            copies.append(pltpu.make_async_copy(
                m2_hbm.at[mlp_layer, :, pl.ds(q * ow, ow)], m2_ref.at[q], wsem.at[MLP_PHASES + q]))
        return copies

    @pl.when(n == 0)
    def _():
        for cp in weight_block_copies():
            cp.start()
        x1buf[...] = jnp.zeros(x1buf.shape, F32)
        for cp in weight_block_copies():
            cp.wait()

    @pl.when(n % tiles_per_seq == 0)
    def _():
        ubuf[:, 0:CONV_HALO, :] = jnp.zeros((groups, CONV_HALO, V7X_LANES), F32)

    @pl.when(n % tiles_per_seq != 0)
    def _():
        ubuf[:, 0:CONV_HALO, :] = ubuf[:, tm:tm + CONV_HALO, :]

    h = _rmsnorm_mod(x_ref[...], gmix_ref[...], modc_ref[0:1, :], modc_ref[1:2, :]).astype(BF16)
    for c0 in range(0, d, V7X_MXU_COLS):
        a = jnp.dot(h, w1_ref[:, c0:c0 + V7X_MXU_COLS], preferred_element_type=F32) + b1_ref[:, c0:c0 + V7X_MXU_COLS]
        gt = (jnp.dot(h, w1_ref[:, d + c0:d + c0 + V7X_MXU_COLS], preferred_element_type=F32)
              + b1_ref[:, d + c0:d + c0 + V7X_MXU_COLS])
        u = a * jax.nn.sigmoid(gt)
        for k in range(V7X_MXU_COLS // V7X_LANES):
            ubuf[c0 // V7X_LANES + k, CONV_HALO:CONV_HALO + tm, :] = u[:, k * V7X_LANES:(k + 1) * V7X_LANES]
    h2buf[...] = _rmsnorm_mod(x1buf[...], gmlp_ref[...], modp_ref[3:4, :], modp_ref[4:5, :]).astype(BF16)

    first_tap = CONV_HALO - (CONV_WIDTH - 1)

    def conv_taps(gi):
        done = None
        for r0 in range(0, tm, CONV_ROWS):
            win = ubuf.at[gi, r0:r0 + CONV_ROWS + CONV_HALO, :]
            tap = jnp.zeros((CONV_ROWS, V7X_LANES), F32) + bdw_ref[gi]
            if done is not None:
                tap = _after(tap, done)
            for t in range(CONV_WIDTH):
                tap = tap + win[pl.ds(first_tap + t, CONV_ROWS, stride=1), :] * wdw_ref[gi, t:t + 1, :]
            cbuf[gi, r0:r0 + CONV_ROWS, :] = tap
            done = tap

    def hidden_phase(p, carry):
        for c0 in range(0, fw, V7X_MXU_COLS):
            t1 = jnp.dot(h2buf[...], m1_ref[p, :, c0:c0 + V7X_MXU_COLS], preferred_element_type=F32)
            t1 = jnp.maximum(t1, 0.0)
            tbuf[p, :, c0:c0 + V7X_MXU_COLS] = (t1 * t1).astype(BF16)
        conv_taps(p)
        return carry

    lax.fori_loop(0, MLP_PHASES, hidden_phase, 0)

    def out_phase(p, carry):
        ff = jnp.dot(tbuf[0], m2_ref[p, 0:fw, :], preferred_element_type=F32)
        for q in range(1, MLP_PHASES):
            ff = ff + jnp.dot(tbuf[q], m2_ref[p, q * fw:(q + 1) * fw, :], preferred_element_type=F32)
        conv_taps(MLP_PHASES + p)
        fbuf[p] = ff
        return carry

    lax.fori_loop(0, MLP_PHASES, out_phase, 0)

    ff = jnp.concatenate([fbuf[p] for p in range(MLP_PHASES)], axis=-1)
    o_ref[...] = x1buf[...] + modp_ref[5:6, :] * ff

    cv = jnp.concatenate([cbuf[gi] for gi in range(groups)], axis=-1)
    mu = jnp.mean(cv, axis=-1, keepdims=True)
    cen = cv - mu
    var = jnp.mean(cen * cen, axis=-1, keepdims=True)
    v = cen * lax.rsqrt(var + EPS) * lng_ref[...] + lnb_ref[...]
    vb = _silu(v).astype(BF16)
    for c0 in range(0, d, V7X_MXU_COLS):
        cols = slice(c0, c0 + V7X_MXU_COLS)
        y = jnp.dot(vb, w2_ref[:, cols], preferred_element_type=F32) + b2_ref[:, cols]
        x1buf[:, cols] = x_ref[:, cols] + modc_ref[2:3, cols] * y


def _conv_mlp_layer(x, mod, gmix, w1, b1, wdw, bdw, lng, lnb, w2, b2, gmlp, m1, m2, conv_layer, mlp_layer):
    b, s, d = x.shape
    d_ff = m1.shape[-1]
    fw = d_ff // MLP_PHASES
    ow = d // MLP_PHASES
    tm = CONV_TM
    tiles_per_seq = s // tm
    n_tiles = b * tiles_per_seq
    groups = d // V7X_LANES
    row = lambda v: v.reshape(1, -1)
    cur = lambda n: jnp.minimum(n, n_tiles - 1)
    prev = lambda n: jnp.maximum(n - 1, 0)
    out = pl.pallas_call(
        functools.partial(_conv_mlp_kernel, tiles_per_seq=tiles_per_seq, mlp_layer=mlp_layer),
        out_shape=jax.ShapeDtypeStruct((b * s, d), F32),
        grid=(n_tiles + 1,),
        in_specs=[
            pl.BlockSpec((tm, d), lambda n: (cur(n), 0)),
            pl.BlockSpec((None, 6, d), lambda n: (cur(n) // tiles_per_seq, 0, 0)),
            pl.BlockSpec((None, 6, d), lambda n: (prev(n) // tiles_per_seq, 0, 0)),
            _resident((1, d)),
            _resident((d, 2 * d), conv_layer),
            _resident((1, 2 * d)),
            _resident((groups, CONV_WIDTH, V7X_LANES)),
            _resident((groups, 1, V7X_LANES)),
            _resident((1, d)),
            _resident((1, d)),
            _resident((d, d), conv_layer),
            _resident((1, d)),
            _resident((1, d)),
            pl.BlockSpec(memory_space=pl.ANY),
            pl.BlockSpec(memory_space=pl.ANY),
        ],
        out_specs=pl.BlockSpec((tm, d), lambda n: (prev(n), 0)),
        scratch_shapes=[
            pltpu.VMEM((groups, CONV_HALO + tm, V7X_LANES), F32),
            pltpu.VMEM((groups, tm, V7X_LANES), F32),
            pltpu.VMEM((tm, d), F32),
            pltpu.VMEM((tm, d), BF16),
            pltpu.VMEM((MLP_PHASES, tm, fw), BF16),
            pltpu.VMEM((MLP_PHASES, tm, ow), F32),
            pltpu.VMEM((MLP_PHASES, d, fw), BF16),
            pltpu.VMEM((MLP_PHASES, d_ff, ow), BF16),
            pltpu.SemaphoreType.DMA((2 * MLP_PHASES,)),
        ],
        compiler_params=pltpu.CompilerParams(
            dimension_semantics=("arbitrary",),
            vmem_limit_bytes=V7X_VMEM_LIMIT_BYTES),
        name="conv_mlp",
    )(x.reshape(b * s, d), mod, mod, row(gmix), w1, row(b1),
      wdw.reshape(CONV_WIDTH, groups, V7X_LANES).transpose(1, 0, 2), bdw.reshape(groups, 1, V7X_LANES),
      row(lng), row(lnb), w2, row(b2), row(gmlp), m1, m2)
    return out.reshape(b, s, d)


def _mlp_kernel(x_ref, mod_ref, g_ref, w1_ref, w2_ref, fg_ref, o_ref, *, final):
    d_ff = w1_ref.shape[1]
    fw = d_ff // MLP_PHASES
    x = x_ref[...]
    h = _rmsnorm_mod(x, g_ref[...], mod_ref[3:4, :], mod_ref[4:5, :]).astype(BF16)
    acc = jnp.zeros(x.shape, F32)
    for c0 in range(0, d_ff, fw):
        t = jnp.maximum(jnp.dot(h, w1_ref[:, c0:c0 + fw], preferred_element_type=F32), 0.0)
        acc = acc + jnp.dot((t * t).astype(BF16), w2_ref[c0:c0 + fw, :], preferred_element_type=F32)
    y = x + mod_ref[5:6, :] * acc
    if final:
        y = y * lax.rsqrt(jnp.mean(y * y, axis=-1, keepdims=True) + EPS) * fg_ref[...]
    o_ref[...] = y


def _mlp_layer(x, mod, g, w1, w2, layer, fg, final):
    b, s, d = x.shape
    d_ff = w1.shape[-1]
    tm = MLP_TM
    return pl.pallas_call(
        functools.partial(_mlp_kernel, final=final),
        out_shape=jax.ShapeDtypeStruct(x.shape, F32),
        grid=(b, s // tm),
        in_specs=[
            pl.BlockSpec((None, tm, d), lambda bi, j: (bi, j, 0)),
            pl.BlockSpec((None, 6, d), lambda bi, j: (bi, 0, 0)),
            _resident((1, d)),
            _resident((d, d_ff), layer),
            _resident((d_ff, d), layer),
            _resident((1, d)),
        ],
        out_specs=pl.BlockSpec((None, tm, d), lambda bi, j: (bi, j, 0)),
        compiler_params=pltpu.CompilerParams(
            dimension_semantics=("arbitrary", "arbitrary"),
            vmem_limit_bytes=V7X_VMEM_LIMIT_BYTES),
        name="mlp_final" if final else "mlp",
    )(x, mod, g.reshape(1, d), w1, w2, fg.reshape(1, d))


def _log_gamma(head):
    return float(np.log(np.float32(1.0) - np.float32(2.0) ** np.float32(-5.0 - head)))


def _ret_kernel(x_ref, xnext_ref, mod_ref, modnext_ref, g_ref, cos_ref, sin_ref, win_ref, gng_ref, gnb_ref,
                wout_ref, o_ref, hbuf, hnext, proj, ybuf, state, dmask, *, tiles_per_seq):
    tm, d = x_ref.shape
    L = RET_L
    heads = RET_HEADS
    dk = d // heads
    dv = 2 * d // heads
    half = dk // 2
    n = pl.program_id(0)

    @pl.when(n == 0)
    def _():
        hnext[...] = _rmsnorm_mod(x_ref[...], g_ref[...], mod_ref[0:1, :], mod_ref[1:2, :]).astype(BF16)
        r = lax.broadcasted_iota(jnp.int32, (L, L), 0)
        c = lax.broadcasted_iota(jnp.int32, (L, L), 1)
        dist = jnp.abs(r - c).astype(F32)
        chunk_shift = CHUNK.bit_length() - 1
        visible = jnp.right_shift(c, chunk_shift) <= jnp.right_shift(r, chunk_shift)
        for hd in range(heads):
            dmask[hd] = jnp.where(visible, jnp.exp(_log_gamma(hd) * dist), 0.0)

    @pl.when(n % tiles_per_seq == 0)
    def _():
        state[...] = jnp.zeros(state.shape, F32)

    hbuf[...] = hnext[...]
    x = x_ref[...]
    h = hbuf[...]
    n_in = win_ref.shape[1]
    for c0 in range(0, n_in, d):
        proj[:, c0:c0 + d] = jnp.dot(h, win_ref[:, c0:c0 + d], preferred_element_type=F32)

    hn = _rmsnorm_mod(xnext_ref[...], g_ref[...], modnext_ref[0:1, :], modnext_ref[1:2, :]).astype(BF16)
    hnext[...] = hn

    cos = cos_ref[...]
    sin = sin_ref[...]
    last = hn[tm - 2 * V7X_SUBLANES:tm, d - V7X_LANES:d].astype(F32)[0:V7X_SUBLANES, :]
    cos = jnp.concatenate([_after(cos[0:V7X_SUBLANES, :], last), cos[V7X_SUBLANES:, :]], axis=0)
    idx = lax.broadcasted_iota(jnp.int32, (L, 1), 0).astype(F32)
    k_off, v_off, g_off = d, 2 * d, 2 * d + heads * dv

    def rope(rows, base):
        x1 = proj[rows, base:base + half]
        x2 = proj[rows, base + half:base + dk]
        c, s = cos[rows, :], sin[rows, :]
        return jnp.concatenate([x1 * c - x2 * s, x2 * c + x1 * s], axis=-1)

    for rows, hd in [(slice(r0, r0 + L), hd) for r0 in range(0, tm, L) for hd in range(heads)]:
        lg = _log_gamma(hd)
        q = rope(rows, hd * dk)
        k = rope(rows, k_off + hd * dk) * (dk ** -0.5)
        vb = proj[rows, v_off + hd * dv:v_off + (hd + 1) * dv].astype(BF16)
        qb = q.astype(BF16)
        scores = lax.dot_general(qb, k.astype(BF16), (((1,), (1,)), ((), ())),
                                 preferred_element_type=F32) * dmask[hd]
        intra = jnp.dot(scores.astype(BF16), vb, preferred_element_type=F32)
        xi = jnp.exp(lg * (idx + 1.0))
        st = state[hd]
        cross = jnp.dot(qb, st.astype(BF16), preferred_element_type=F32) * xi
        zeta = jnp.exp(lg * (float(L - 1) - idx))
        kz = (k * zeta).astype(BF16)
        state[hd] = st * float(np.exp(np.float32(lg) * np.float32(L))) + lax.dot_general(
            kz, vb, (((0,), (0,)), ((), ())), preferred_element_type=F32)
        y = intra + cross
        mu = jnp.mean(y, axis=-1, keepdims=True)
        cen = y - mu
        var = jnp.mean(cen * cen, axis=-1, keepdims=True)
        yn = cen * lax.rsqrt(var + EPS) * gng_ref[hd:hd + 1, :] + gnb_ref[hd:hd + 1, :]
        gate = proj[rows, g_off + hd * dv:g_off + (hd + 1) * dv]
        ybuf[rows, hd * dv:(hd + 1) * dv] = (_silu(gate) * yn).astype(BF16)

    out = jnp.dot(ybuf[...], wout_ref[...], preferred_element_type=F32)
    o_ref[...] = x + mod_ref[2:3, :] * out


def _ret_layer(x, mod, g, cos, sin, w_in, gn_g, gn_b, w_out, ret_layer):
    b, s, d = x.shape
    L, tm = RET_L, RET_TM
    heads = RET_HEADS
    dk, dv = d // heads, 2 * d // heads
    n_in = w_in.shape[-1]
    tiles_per_seq = s // tm
    n_tiles = b * tiles_per_seq
    nxt = lambda n: jnp.minimum(n + 1, n_tiles - 1)
    x2 = x.reshape(b * s, d)
    out = pl.pallas_call(
        functools.partial(_ret_kernel, tiles_per_seq=tiles_per_seq),
        out_shape=jax.ShapeDtypeStruct((b * s, d), F32),
        grid=(n_tiles,),
        in_specs=[
            pl.BlockSpec((tm, d), lambda n: (n, 0)),
            pl.BlockSpec((tm, d), lambda n: (nxt(n), 0)),
            pl.BlockSpec((None, 6, d), lambda n: (n // tiles_per_seq, 0, 0)),
            pl.BlockSpec((None, 6, d), lambda n: (nxt(n) // tiles_per_seq, 0, 0)),
            _resident((1, d)),
            pl.BlockSpec((tm, dk // 2), lambda n: (n % tiles_per_seq, 0)),
            pl.BlockSpec((tm, dk // 2), lambda n: (n % tiles_per_seq, 0)),
            _resident((d, n_in), ret_layer),
            _resident((heads, dv)),
            _resident((heads, dv)),
            _resident((heads * dv, d), ret_layer),
        ],
        out_specs=pl.BlockSpec((tm, d), lambda n: (n, 0)),
        scratch_shapes=[
            pltpu.VMEM((tm, d), BF16),
            pltpu.VMEM((tm, d), BF16),
            pltpu.VMEM((tm, n_in), F32),
            pltpu.VMEM((tm, heads * dv), BF16),
            pltpu.VMEM((heads, dk, dv), F32),
            pltpu.VMEM((heads, L, L), F32),
        ],
        compiler_params=pltpu.CompilerParams(
            dimension_semantics=("arbitrary",),
            vmem_limit_bytes=V7X_VMEM_LIMIT_BYTES),
        name="retention_mixer",
    )(x2, x2, mod, mod, g.reshape(1, d), cos, sin, w_in, gn_g, gn_b, w_out)
    return out.reshape(b, s, d)


def _rope_tables(seq, dk):
    pos = jnp.arange(seq, dtype=F32)
    inv = ROPE_BASE ** (-jnp.arange(0, dk, 2, dtype=F32) / dk)
    ang = pos[:, None] * inv[None, :]
    return jnp.cos(ang), jnp.sin(ang)


def kernel(x, c, ada_w, ada_b, norm_mix_g, norm_mlp_g, conv_w_pw1, conv_b_pw1, conv_w_dw, conv_b_dw, conv_ln_g, conv_ln_b, conv_w_pw2, conv_b_pw2, ret_w_in, ret_gn_g, ret_gn_b, ret_w_out, mlp_w1, mlp_w2, final_norm_g):
    depth = ada_w.shape[0]
    b, s, d = x.shape
    assert s % CONV_TM == 0 and s % MLP_TM == 0 and s % RET_TM == 0 and RET_TM % RET_L == 0 and RET_L % CHUNK == 0
    assert CONV_HALO >= CONV_WIDTH - 1 and CONV_HALO % V7X_SUBLANES == 0 and CONV_TM % CONV_ROWS == 0
    assert d == 2 * MLP_PHASES * V7X_LANES

    mod = _ada(c, ada_w, ada_b).reshape(depth, b, 6, d)
    cos, sin = _rope_tables(s, d // RET_HEADS)
    pw1, pw2 = conv_w_pw1.astype(BF16), conv_w_pw2.astype(BF16)
    w_in, w_out = ret_w_in.astype(BF16), ret_w_out.astype(BF16)
    m1, m2 = mlp_w1.astype(BF16), mlp_w2.astype(BF16)
    for i in range(depth):
        jm = i // 2
        if i % 2 == 0:
            assert i != depth - 1
            x = _conv_mlp_layer(x, mod[i], norm_mix_g[i], pw1, conv_b_pw1[jm], conv_w_dw[jm], conv_b_dw[jm],
                                conv_ln_g[jm], conv_ln_b[jm], pw2, conv_b_pw2[jm], norm_mlp_g[i],
                                m1, m2, conv_layer=jm, mlp_layer=i)
        else:
            x = _ret_layer(x, mod[i], norm_mix_g[i], cos, sin, w_in, ret_gn_g[jm], ret_gn_b[jm], w_out, ret_layer=jm)
            x = _mlp_layer(x, mod[i], norm_mlp_g[i], m1, m2, i, final_norm_g, final=(i == depth - 1))
    return x
```

```python
import functools

import numpy as np
import jax
import jax.numpy as jnp
from jax import lax
from jax.experimental import pallas as pl
from jax.experimental.pallas import tpu as pltpu

F32 = jnp.float32
BF16 = jnp.bfloat16

EPS = 1e-6
CHUNK = 64
CONV_WIDTH = 31
RET_HEADS = 4
ROPE_BASE = 10000.0

V7X_SUBLANES = 8
V7X_LANES = 128
V7X_MXU_COLS = 256
V7X_VMEM_LIMIT_BYTES = 56 * 1024 * 1024

CONV_TM = 512
CONV_HALO = 32
CONV_ROWS = 16
MLP_PHASES = 4
MLP_TM = 1024
RET_L = 256


def _resident(shape, layer=None):
    zeros = (0,) * len(shape)
    if layer is None:
        return pl.BlockSpec(shape, lambda *_: zeros, pipeline_mode=pl.Buffered(1))
    return pl.BlockSpec((None,) + tuple(shape), lambda *_: (layer,) + zeros, pipeline_mode=pl.Buffered(1))


def _rmsnorm_mod(x, g, shift, scale):
    y = x * lax.rsqrt(jnp.mean(x * x, axis=-1, keepdims=True) + EPS)
    return (y * g) * (1.0 + scale) + shift


def _silu(v):
    return v * jax.nn.sigmoid(v)


def _after(value, dep):
    bits = pltpu.bitcast(dep, jnp.uint32)
    zero = lax.shift_right_logical(lax.shift_right_logical(bits, jnp.uint32(16)), jnp.uint32(16))
    return pltpu.bitcast(pltpu.bitcast(value, jnp.uint32) + zero, F32)


def _ada_kernel(c_ref, w_ref, b_ref, o_ref):
    cond = _silu(c_ref[...])
    o_ref[...] = jnp.dot(cond.astype(BF16), w_ref[...].astype(BF16),
                         preferred_element_type=F32) + b_ref[...]


def _ada(c, ada_w, ada_b):
    depth, d, n = ada_w.shape
    b = c.shape[0]
    tn = n // 4
    return pl.pallas_call(
        _ada_kernel,
        out_shape=jax.ShapeDtypeStruct((depth, b, n), F32),
        grid=(depth, n // tn),
        in_specs=[
            pl.BlockSpec((b, d), lambda l, j: (0, 0)),
            pl.BlockSpec((None, d, tn), lambda l, j: (l, 0, j)),
            pl.BlockSpec((None, 1, tn), lambda l, j: (l, 0, j)),
        ],
        out_specs=pl.BlockSpec((None, b, tn), lambda l, j: (l, 0, j)),
        compiler_params=pltpu.CompilerParams(
            dimension_semantics=("arbitrary", "arbitrary"),
            vmem_limit_bytes=V7X_VMEM_LIMIT_BYTES),
        name="ada_mod",
    )(c, ada_w, ada_b.reshape(depth, 1, n))


def _conv_mlp_kernel(*refs, tiles_per_seq, mlp_layer, side):
    (x_ref, modc_ref, modp_ref, gmix_ref, w1_ref, b1_ref, wdw_ref, bdw_ref,
     lng_ref, lnb_ref, w2_ref, b2_ref, gmlp_ref, m1_hbm, m2_hbm) = refs[:15]
    ns = len(side)
    side_src = refs[15:15 + ns]
    o_ref = refs[15 + ns]
    side_dst = refs[16 + ns:16 + 2 * ns]
    ubuf, cbuf, x1buf, h2buf, tbuf, fbuf, m1_ref, m2_ref, wsem = refs[16 + 2 * ns:25 + 2 * ns]
    stage32 = refs[25 + 2 * ns:25 + 3 * ns]
    stage16 = refs[25 + 3 * ns:25 + 4 * ns]
    csem = refs[25 + 4 * ns] if ns else None
    tm, d = x_ref.shape
    groups = d // V7X_LANES
    fw = m1_ref.shape[2]
    ow = m2_ref.shape[2]
    n = pl.program_id(0)
    n_chunks = pl.num_programs(0) - 1

    def side_rows(k, j):
        first, count = side[k]
        rows = stage32[k].shape[1]
        return slice(first, first + count), pl.ds(pl.multiple_of(j * rows, rows), rows)

    def side_in(k, j):
        layers, rows = side_rows(k, j)
        return pltpu.make_async_copy(side_src[k].at[layers, rows, :], stage32[k], csem.at[k])

    def side_out(k, j):
        _, rows = side_rows(k, j)
        return pltpu.make_async_copy(stage16[k], side_dst[k].at[:, rows, :], csem.at[ns + k])

    if ns:
        @pl.when(n < n_chunks)
        def _():
            for k in range(ns):
                side_in(k, n).start()

    def weight_block_copies():
        copies = []
        for q in range(MLP_PHASES):
            copies.append(pltpu.make_async_copy(
                m1_hbm.at[mlp_layer, :, pl.ds(q * fw, fw)], m1_ref.at[q], wsem.at[q]))
            copies.append(pltpu.make_async_copy(
                m2_hbm.at[mlp_layer, :, pl.ds(q * ow, ow)], m2_ref.at[q], wsem.at[MLP_PHASES + q]))
        return copies

    @pl.when(n == 0)
    def _():
        for cp in weight_block_copies():
            cp.start()
        x1buf[...] = jnp.zeros(x1buf.shape, F32)
        for cp in weight_block_copies():
            cp.wait()

    @pl.when(n % tiles_per_seq == 0)
    def _():
        ubuf[:, 0:CONV_HALO, :] = jnp.zeros((groups, CONV_HALO, V7X_LANES), F32)

    @pl.when(n % tiles_per_seq != 0)
    def _():
        ubuf[:, 0:CONV_HALO, :] = ubuf[:, tm:tm + CONV_HALO, :]

    h = _rmsnorm_mod(x_ref[...], gmix_ref[...], modc_ref[0:1, :], modc_ref[1:2, :]).astype(BF16)
    for c0 in range(0, d, V7X_MXU_COLS):
        a = jnp.dot(h, w1_ref[:, c0:c0 + V7X_MXU_COLS], preferred_element_type=F32) + b1_ref[:, c0:c0 + V7X_MXU_COLS]
        gt = (jnp.dot(h, w1_ref[:, d + c0:d + c0 + V7X_MXU_COLS], preferred_element_type=F32)
              + b1_ref[:, d + c0:d + c0 + V7X_MXU_COLS])
        u = a * jax.nn.sigmoid(gt)
        for k in range(V7X_MXU_COLS // V7X_LANES):
            ubuf[c0 // V7X_LANES + k, CONV_HALO:CONV_HALO + tm, :] = u[:, k * V7X_LANES:(k + 1) * V7X_LANES]
    h2buf[...] = _rmsnorm_mod(x1buf[...], gmlp_ref[...], modp_ref[3:4, :], modp_ref[4:5, :]).astype(BF16)

    first_tap = CONV_HALO - (CONV_WIDTH - 1)

    def conv_taps(gi):
        done = None
        for r0 in range(0, tm, CONV_ROWS):
            win = ubuf.at[gi, r0:r0 + CONV_ROWS + CONV_HALO, :]
            tap = jnp.zeros((CONV_ROWS, V7X_LANES), F32) + bdw_ref[gi]
            if done is not None:
                tap = _after(tap, done)
            for t in range(CONV_WIDTH):
                tap = tap + win[pl.ds(first_tap + t, CONV_ROWS, stride=1), :] * wdw_ref[gi, t:t + 1, :]
            cbuf[gi, r0:r0 + CONV_ROWS, :] = tap
            done = tap

    def hidden_phase(p, carry):
        for c0 in range(0, fw, V7X_MXU_COLS):
            t1 = jnp.dot(h2buf[...], m1_ref[p, :, c0:c0 + V7X_MXU_COLS], preferred_element_type=F32)
            t1 = jnp.maximum(t1, 0.0)
            tbuf[p, :, c0:c0 + V7X_MXU_COLS] = (t1 * t1).astype(BF16)
        conv_taps(p)
        return carry

    lax.fori_loop(0, MLP_PHASES, hidden_phase, 0)

    def out_phase(p, carry):
        ff = jnp.dot(tbuf[0], m2_ref[p, 0:fw, :], preferred_element_type=F32)
        for q in range(1, MLP_PHASES):
            ff = ff + jnp.dot(tbuf[q], m2_ref[p, q * fw:(q + 1) * fw, :], preferred_element_type=F32)
        conv_taps(MLP_PHASES + p)
        fbuf[p] = ff
        return carry

    lax.fori_loop(0, MLP_PHASES, out_phase, 0)

    ff = jnp.concatenate([fbuf[p] for p in range(MLP_PHASES)], axis=-1)
    o_ref[...] = x1buf[...] + modp_ref[5:6, :] * ff

    cv = jnp.concatenate([cbuf[gi] for gi in range(groups)], axis=-1)
    mu = jnp.mean(cv, axis=-1, keepdims=True)
    cen = cv - mu
    var = jnp.mean(cen * cen, axis=-1, keepdims=True)
    v = cen * lax.rsqrt(var + EPS) * lng_ref[...] + lnb_ref[...]
    vb = _silu(v).astype(BF16)
    for c0 in range(0, d, V7X_MXU_COLS):
        cols = slice(c0, c0 + V7X_MXU_COLS)
        y = jnp.dot(vb, w2_ref[:, cols], preferred_element_type=F32) + b2_ref[:, cols]
        x1buf[:, cols] = x_ref[:, cols] + modc_ref[2:3, cols] * y

    if ns:
        @pl.when(n >= 1)
        def _():
            for k in range(ns):
                side_out(k, n - 1).wait()

        @pl.when(n < n_chunks)
        def _():
            for k in range(ns):
                side_in(k, n).wait()
                stage16[k][...] = stage32[k][...].astype(BF16)
                side_out(k, n).start()


def _conv_mlp_layer(x, mod, gmix, w1, b1, wdw, bdw, lng, lnb, w2, b2, gmlp, m1, m2, conv_layer, mlp_layer, side=()):
    b, s, d = x.shape
    d_ff = m1.shape[-1]
    fw = d_ff // MLP_PHASES
    ow = d // MLP_PHASES
    tm = CONV_TM
    tiles_per_seq = s // tm
    n_tiles = b * tiles_per_seq
    groups = d // V7X_LANES
    row = lambda v: v.reshape(1, -1)
    cur = lambda n: jnp.minimum(n, n_tiles - 1)
    prev = lambda n: jnp.maximum(n - 1, 0)
    side = [job for job in side if job[2] > 0]
    bf16_rows = 2 * V7X_SUBLANES
    for src, _, _ in side:
        assert src.shape[1] % (n_tiles * bf16_rows) == 0
    stage = lambda dtype: [pltpu.VMEM((count, src.shape[1] // n_tiles, src.shape[2]), dtype)
                           for src, _, count in side]
    any_spec = pl.BlockSpec(memory_space=pl.ANY)
    outs = pl.pallas_call(
        functools.partial(_conv_mlp_kernel, tiles_per_seq=tiles_per_seq, mlp_layer=mlp_layer,
                          side=tuple((first, count) for _, first, count in side)),
        out_shape=[jax.ShapeDtypeStruct((b * s, d), F32)]
                  + [jax.ShapeDtypeStruct((count,) + src.shape[1:], BF16) for src, _, count in side],
        grid=(n_tiles + 1,),
        in_specs=[
            pl.BlockSpec((tm, d), lambda n: (cur(n), 0)),
            pl.BlockSpec((None, 6, d), lambda n: (cur(n) // tiles_per_seq, 0, 0)),
            pl.BlockSpec((None, 6, d), lambda n: (prev(n) // tiles_per_seq, 0, 0)),
            _resident((1, d)),
            _resident((d, 2 * d), conv_layer),
            _resident((1, 2 * d)),
            _resident((groups, CONV_WIDTH, V7X_LANES)),
            _resident((groups, 1, V7X_LANES)),
            _resident((1, d)),
            _resident((1, d)),
            _resident((d, d), conv_layer),
            _resident((1, d)),
            _resident((1, d)),
            any_spec,
            any_spec,
        ] + [any_spec] * len(side),
        out_specs=[pl.BlockSpec((tm, d), lambda n: (prev(n), 0))] + [any_spec] * len(side),
        scratch_shapes=[
            pltpu.VMEM((groups, CONV_HALO + tm, V7X_LANES), F32),
            pltpu.VMEM((groups, tm, V7X_LANES), F32),
            pltpu.VMEM((tm, d), F32),
            pltpu.VMEM((tm, d), BF16),
            pltpu.VMEM((MLP_PHASES, tm, fw), BF16),
            pltpu.VMEM((MLP_PHASES, tm, ow), F32),
            pltpu.VMEM((MLP_PHASES, d, fw), BF16),
            pltpu.VMEM((MLP_PHASES, d_ff, ow), BF16),
            pltpu.SemaphoreType.DMA((2 * MLP_PHASES,)),
        ] + stage(F32) + stage(BF16) + ([pltpu.SemaphoreType.DMA((2 * len(side),))] if side else []),
        compiler_params=pltpu.CompilerParams(
            dimension_semantics=("arbitrary",),
            vmem_limit_bytes=V7X_VMEM_LIMIT_BYTES),
        name="conv_mlp",
    )(x.reshape(b * s, d), mod, mod, row(gmix), w1, row(b1),
      wdw.reshape(CONV_WIDTH, groups, V7X_LANES).transpose(1, 0, 2), bdw.reshape(groups, 1, V7X_LANES),
      row(lng), row(lnb), w2, row(b2), row(gmlp), m1, m2, *[src for src, _, _ in side])
    return outs[0].reshape(b, s, d), list(outs[1:])


def _mlp_kernel(x_ref, mod_ref, g_ref, w1_ref, w2_ref, fg_ref, o_ref, *, final):
    d_ff = w1_ref.shape[1]
    fw = d_ff // MLP_PHASES
    x = x_ref[...]
    h = _rmsnorm_mod(x, g_ref[...], mod_ref[3:4, :], mod_ref[4:5, :]).astype(BF16)
    acc = jnp.zeros(x.shape, F32)
    for c0 in range(0, d_ff, fw):
        t = jnp.maximum(jnp.dot(h, w1_ref[:, c0:c0 + fw], preferred_element_type=F32), 0.0)
        acc = acc + jnp.dot((t * t).astype(BF16), w2_ref[c0:c0 + fw, :], preferred_element_type=F32)
    y = x + mod_ref[5:6, :] * acc
    if final:
        y = y * lax.rsqrt(jnp.mean(y * y, axis=-1, keepdims=True) + EPS) * fg_ref[...]
    o_ref[...] = y


def _mlp_layer(x, mod, g, w1, w2, layer, fg, final):
    b, s, d = x.shape
    d_ff = w1.shape[-1]
    tm = MLP_TM
    return pl.pallas_call(
        functools.partial(_mlp_kernel, final=final),
        out_shape=jax.ShapeDtypeStruct(x.shape, F32),
        grid=(b, s // tm),
        in_specs=[
            pl.BlockSpec((None, tm, d), lambda bi, j: (bi, j, 0)),
            pl.BlockSpec((None, 6, d), lambda bi, j: (bi, 0, 0)),
            _resident((1, d)),
            _resident((d, d_ff), layer),
            _resident((d_ff, d), layer),
            _resident((1, d)),
        ],
        out_specs=pl.BlockSpec((None, tm, d), lambda bi, j: (bi, j, 0)),
        compiler_params=pltpu.CompilerParams(
            dimension_semantics=("arbitrary", "arbitrary"),
            vmem_limit_bytes=V7X_VMEM_LIMIT_BYTES),
        name="mlp_final" if final else "mlp",
    )(x, mod, g.reshape(1, d), w1, w2, fg.reshape(1, d))


def _log_gamma(head):
    return float(np.log(np.float32(1.0) - np.float32(2.0) ** np.float32(-5.0 - head)))


def _ret_kernel(x_ref, xnext_ref, mod_ref, modnext_ref, g_ref, cos_ref, sin_ref, win_ref, gng_ref, gnb_ref,
                wout_ref, o_ref, hbuf, hnext, proj, ybuf, state, dmask, *, tiles_per_seq):
    L, d = x_ref.shape
    heads = RET_HEADS
    dk = d // heads
    dv = 2 * d // heads
    half = dk // 2
    n = pl.program_id(0)

    @pl.when(n == 0)
    def _():
        hnext[...] = _rmsnorm_mod(x_ref[...], g_ref[...], mod_ref[0:1, :], mod_ref[1:2, :]).astype(BF16)
        r = lax.broadcasted_iota(jnp.int32, (L, L), 0)
        c = lax.broadcasted_iota(jnp.int32, (L, L), 1)
        dist = jnp.abs(r - c).astype(F32)
        chunk_shift = CHUNK.bit_length() - 1
        visible = jnp.right_shift(c, chunk_shift) <= jnp.right_shift(r, chunk_shift)
        for hd in range(heads):
            dmask[hd] = jnp.where(visible, jnp.exp(_log_gamma(hd) * dist), 0.0)

    @pl.when(n % tiles_per_seq == 0)
    def _():
        state[...] = jnp.zeros(state.shape, F32)

    hbuf[...] = hnext[...]
    x = x_ref[...]
    h = hbuf[...]
    n_in = win_ref.shape[1]
    for c0 in range(0, n_in, d):
        proj[:, c0:c0 + d] = jnp.dot(h, win_ref[:, c0:c0 + d], preferred_element_type=F32)

    hn = _rmsnorm_mod(xnext_ref[...], g_ref[...], modnext_ref[0:1, :], modnext_ref[1:2, :]).astype(BF16)
    hnext[...] = hn

    cos = cos_ref[...]
    sin = sin_ref[...]
    last = hn[L - 2 * V7X_SUBLANES:L, d - V7X_LANES:d].astype(F32)[0:V7X_SUBLANES, :]
    cos = jnp.concatenate([_after(cos[0:V7X_SUBLANES, :], last), cos[V7X_SUBLANES:, :]], axis=0)
    idx = lax.broadcasted_iota(jnp.int32, (L, 1), 0).astype(F32)
    k_off, v_off, g_off = d, 2 * d, 2 * d + heads * dv

    def rope(base):
        x1 = proj[:, base:base + half]
        x2 = proj[:, base + half:base + dk]
        return jnp.concatenate([x1 * cos - x2 * sin, x2 * cos + x1 * sin], axis=-1)

    for hd in range(heads):
        lg = _log_gamma(hd)
        q = rope(hd * dk)
        k = rope(k_off + hd * dk) * (dk ** -0.5)
        vb = proj[:, v_off + hd * dv:v_off + (hd + 1) * dv].astype(BF16)
        qb = q.astype(BF16)
        scores = lax.dot_general(qb, k.astype(BF16), (((1,), (1,)), ((), ())),
                                 preferred_element_type=F32) * dmask[hd]
        intra = jnp.dot(scores.astype(BF16), vb, preferred_element_type=F32)
        xi = jnp.exp(lg * (idx + 1.0))
        st = state[hd]
        cross = jnp.dot(qb, st.astype(BF16), preferred_element_type=F32) * xi
        zeta = jnp.exp(lg * (float(L - 1) - idx))
        kz = (k * zeta).astype(BF16)
        state[hd] = st * float(np.exp(np.float32(lg) * np.float32(L))) + lax.dot_general(
            kz, vb, (((0,), (0,)), ((), ())), preferred_element_type=F32)
        y = intra + cross
        mu = jnp.mean(y, axis=-1, keepdims=True)
        cen = y - mu
        var = jnp.mean(cen * cen, axis=-1, keepdims=True)
        yn = cen * lax.rsqrt(var + EPS) * gng_ref[hd:hd + 1, :] + gnb_ref[hd:hd + 1, :]
        gate = proj[:, g_off + hd * dv:g_off + (hd + 1) * dv]
        ybuf[:, hd * dv:(hd + 1) * dv] = (_silu(gate) * yn).astype(BF16)

    out = jnp.dot(ybuf[...], wout_ref[...], preferred_element_type=F32)
    o_ref[...] = x + mod_ref[2:3, :] * out


def _ret_layer(x, mod, g, cos, sin, w_in, gn_g, gn_b, w_out, ret_layer):
    b, s, d = x.shape
    L = RET_L
    heads = RET_HEADS
    dk, dv = d // heads, 2 * d // heads
    n_in = w_in.shape[-1]
    tiles_per_seq = s // L
    n_tiles = b * tiles_per_seq
    nxt = lambda n: jnp.minimum(n + 1, n_tiles - 1)
    x2 = x.reshape(b * s, d)
    out = pl.pallas_call(
        functools.partial(_ret_kernel, tiles_per_seq=tiles_per_seq),
        out_shape=jax.ShapeDtypeStruct((b * s, d), F32),
        grid=(n_tiles,),
        in_specs=[
            pl.BlockSpec((L, d), lambda n: (n, 0)),
            pl.BlockSpec((L, d), lambda n: (nxt(n), 0)),
            pl.BlockSpec((None, 6, d), lambda n: (n // tiles_per_seq, 0, 0)),
            pl.BlockSpec((None, 6, d), lambda n: (nxt(n) // tiles_per_seq, 0, 0)),
            _resident((1, d)),
            pl.BlockSpec((L, dk // 2), lambda n: (n % tiles_per_seq, 0)),
            pl.BlockSpec((L, dk // 2), lambda n: (n % tiles_per_seq, 0)),
            _resident((d, n_in), ret_layer),
            _resident((heads, dv)),
            _resident((heads, dv)),
            _resident((heads * dv, d), ret_layer),
        ],
        out_specs=pl.BlockSpec((L, d), lambda n: (n, 0)),
        scratch_shapes=[
            pltpu.VMEM((L, d), BF16),
            pltpu.VMEM((L, d), BF16),
            pltpu.VMEM((L, n_in), F32),
            pltpu.VMEM((L, heads * dv), BF16),
            pltpu.VMEM((heads, dk, dv), F32),
            pltpu.VMEM((heads, L, L), F32),
        ],
        compiler_params=pltpu.CompilerParams(
            dimension_semantics=("arbitrary",),
            vmem_limit_bytes=V7X_VMEM_LIMIT_BYTES),
        name="retention_mixer",
    )(x2, x2, mod, mod, g.reshape(1, d), cos, sin, w_in, gn_g, gn_b, w_out)
    return out.reshape(b, s, d)


def _rope_tables(seq, dk):
    pos = jnp.arange(seq, dtype=F32)
    inv = ROPE_BASE ** (-jnp.arange(0, dk, 2, dtype=F32) / dk)
    ang = pos[:, None] * inv[None, :]
    return jnp.cos(ang), jnp.sin(ang)


def kernel(x, c, ada_w, ada_b, norm_mix_g, norm_mlp_g, conv_w_pw1, conv_b_pw1, conv_w_dw, conv_b_dw, conv_ln_g, conv_ln_b, conv_w_pw2, conv_b_pw2, ret_w_in, ret_gn_g, ret_gn_b, ret_w_out, mlp_w1, mlp_w2, final_norm_g):
    depth = ada_w.shape[0]
    b, s, d = x.shape
    assert s % CONV_TM == 0 and s % MLP_TM == 0 and s % RET_L == 0 and RET_L % CHUNK == 0
    assert CONV_HALO >= CONV_WIDTH - 1 and CONV_HALO % V7X_SUBLANES == 0 and CONV_TM % CONV_ROWS == 0
    assert d == 2 * MLP_PHASES * V7X_LANES

    mod = _ada(c, ada_w, ada_b).reshape(depth, b, 6, d)
    cos, sin = _rope_tables(s, d // RET_HEADS)
    stacks = {"pw1": conv_w_pw1, "pw2": conv_w_pw2, "w_in": ret_w_in, "w_out": ret_w_out, "m1": mlp_w1, "m2": mlp_w2}
    first_later = {"pw1": 1, "pw2": 1, "w_in": 0, "w_out": 0, "m1": 1, "m2": 1}
    jobs = {name: (w, first_later[name], w.shape[0] - first_later[name]) for name, w in stacks.items()}
    names = [name for name, job in jobs.items() if job[2] > 0]
    assert depth % 2 == 0
    for i in range(depth):
        jm = i // 2
        if i == 0:
            own = [stacks[name][0:1].astype(BF16) for name in ("pw1", "pw2", "m1", "m2")]
            x, copies = _conv_mlp_layer(x, mod[i], norm_mix_g[i], own[0], conv_b_pw1[jm], conv_w_dw[jm], conv_b_dw[jm],
                                        conv_ln_g[jm], conv_ln_b[jm], own[1], conv_b_pw2[jm], norm_mlp_g[i],
                                        own[2], own[3], conv_layer=0, mlp_layer=0, side=[jobs[name] for name in names])
            later = dict(zip(names, copies))
            pw1, pw2, w_in, w_out, m1, m2 = (later.get(name) for name in stacks)
        elif i % 2 == 0:
            x, _ = _conv_mlp_layer(x, mod[i], norm_mix_g[i], pw1, conv_b_pw1[jm], conv_w_dw[jm], conv_b_dw[jm],
                                   conv_ln_g[jm], conv_ln_b[jm], pw2, conv_b_pw2[jm], norm_mlp_g[i],
                                   m1, m2, conv_layer=jm - 1, mlp_layer=i - 1)
        else:
            x = _ret_layer(x, mod[i], norm_mix_g[i], cos, sin, w_in, ret_gn_g[jm], ret_gn_b[jm], w_out, ret_layer=jm)
            x = _mlp_layer(x, mod[i], norm_mlp_g[i], m1, m2, i - 1, final_norm_g, final=(i == depth - 1))
    return x
```

```python
import functools

import numpy as np
import jax
import jax.numpy as jnp
from jax import lax
from jax.experimental import pallas as pl
from jax.experimental.pallas import tpu as pltpu

F32 = jnp.float32
BF16 = jnp.bfloat16

EPS = 1e-6
CHUNK = 64
CONV_WIDTH = 31
RET_HEADS = 4
ROPE_BASE = 10000.0

V7X_SUBLANES = 8
V7X_LANES = 128
V7X_MXU_COLS = 256
V7X_VMEM_LIMIT_BYTES = 56 * 1024 * 1024

CONV_TM = 512
CONV_HALO = 32
CONV_ROWS = 16
MLP_PHASES = 4
MLP_TM = 1024
RET_L = 256


def _resident(shape, layer=None):
    zeros = (0,) * len(shape)
    if layer is None:
        return pl.BlockSpec(shape, lambda *_: zeros, pipeline_mode=pl.Buffered(1))
    return pl.BlockSpec((None,) + tuple(shape), lambda *_: (layer,) + zeros, pipeline_mode=pl.Buffered(1))


def _rmsnorm_mod(x, g, shift, scale):
    y = x * lax.rsqrt(jnp.mean(x * x, axis=-1, keepdims=True) + EPS)
    return (y * g) * (1.0 + scale) + shift


def _silu(v):
    return v * jax.nn.sigmoid(v)


def _after(value, dep):
    bits = pltpu.bitcast(dep, jnp.uint32)
    zero = lax.shift_right_logical(lax.shift_right_logical(bits, jnp.uint32(16)), jnp.uint32(16))
    return pltpu.bitcast(pltpu.bitcast(value, jnp.uint32) + zero, F32)


def _ada_kernel(c_ref, w_ref, b_ref, o_ref):
    cond = _silu(c_ref[...])
    o_ref[...] = jnp.dot(cond.astype(BF16), w_ref[...].astype(BF16),
                         preferred_element_type=F32) + b_ref[...]


def _ada(c, ada_w, ada_b):
    depth, d, n = ada_w.shape
    b = c.shape[0]
    tn = n // 4
    return pl.pallas_call(
        _ada_kernel,
        out_shape=jax.ShapeDtypeStruct((depth, b, n), F32),
        grid=(depth, n // tn),
        in_specs=[
            pl.BlockSpec((b, d), lambda l, j: (0, 0)),
            pl.BlockSpec((None, d, tn), lambda l, j: (l, 0, j)),
            pl.BlockSpec((None, 1, tn), lambda l, j: (l, 0, j)),
        ],
        out_specs=pl.BlockSpec((None, b, tn), lambda l, j: (l, 0, j)),
        compiler_params=pltpu.CompilerParams(
            dimension_semantics=("arbitrary", "arbitrary"),
            vmem_limit_bytes=V7X_VMEM_LIMIT_BYTES),
        name="ada_mod",
    )(c, ada_w, ada_b.reshape(depth, 1, n))


def _conv_mlp_kernel(*refs, tiles_per_seq, mlp_layer, side):
    (x_ref, modc_ref, modp_ref, gmix_ref, w1_ref, b1_ref, wdw_ref, bdw_ref,
     lng_ref, lnb_ref, w2_ref, b2_ref, gmlp_ref, m1_hbm, m2_hbm) = refs[:15]
    ns = len(side)
    side_src = refs[15:15 + ns]
    o_ref = refs[15 + ns]
    side_dst = refs[16 + ns:16 + 2 * ns]
    ubuf, cbuf, x1buf, h2buf, tbuf, fbuf, m1_ref, m2_ref, wsem = refs[16 + 2 * ns:25 + 2 * ns]
    stage32 = refs[25 + 2 * ns:25 + 3 * ns]
    stage16 = refs[25 + 3 * ns:25 + 4 * ns]
    csem = refs[25 + 4 * ns] if ns else None
    tm, d = x_ref.shape
    groups = d // V7X_LANES
    fw = m1_ref.shape[2]
    ow = m2_ref.shape[2]
    n = pl.program_id(0)
    n_chunks = pl.num_programs(0) - 1

    def side_rows(k, j):
        first, count = side[k]
        rows = stage32[k].shape[1]
        return slice(first, first + count), pl.ds(pl.multiple_of(j * rows, rows), rows)

    def side_in(k, j):
        layers, rows = side_rows(k, j)
        return pltpu.make_async_copy(side_src[k].at[layers, rows, :], stage32[k], csem.at[k])

    def side_out(k, j):
        _, rows = side_rows(k, j)
        return pltpu.make_async_copy(stage16[k], side_dst[k].at[:, rows, :], csem.at[ns + k])

    if ns:
        @pl.when(n < n_chunks)
        def _():
            for k in range(ns):
                side_in(k, n).start()

    def weight_block_copies():
        copies = []
        for q in range(MLP_PHASES):
            copies.append(pltpu.make_async_copy(
                m1_hbm.at[mlp_layer, :, pl.ds(q * fw, fw)], m1_ref.at[q], wsem.at[q]))
            copies.append(pltpu.make_async_copy(
                m2_hbm.at[mlp_layer, :, pl.ds(q * ow, ow)], m2_ref.at[q], wsem.at[MLP_PHASES + q]))
        return copies

    def mixer_first():
        @pl.when(n % tiles_per_seq == 0)
        def _():
            ubuf[:, 0:CONV_HALO, :] = jnp.zeros((groups, CONV_HALO, V7X_LANES), F32)

        @pl.when(n % tiles_per_seq != 0)
        def _():
            ubuf[:, 0:CONV_HALO, :] = ubuf[:, tm:tm + CONV_HALO, :]

        h = _rmsnorm_mod(x_ref[...], gmix_ref[...], modc_ref[0:1, :], modc_ref[1:2, :]).astype(BF16)
        for c0 in range(0, d, V7X_MXU_COLS):
            a = jnp.dot(h, w1_ref[:, c0:c0 + V7X_MXU_COLS], preferred_element_type=F32) + b1_ref[:, c0:c0 + V7X_MXU_COLS]
            gt = (jnp.dot(h, w1_ref[:, d + c0:d + c0 + V7X_MXU_COLS], preferred_element_type=F32)
                  + b1_ref[:, d + c0:d + c0 + V7X_MXU_COLS])
            u = a * jax.nn.sigmoid(gt)
            for k in range(V7X_MXU_COLS // V7X_LANES):
                ubuf[c0 // V7X_LANES + k, CONV_HALO:CONV_HALO + tm, :] = u[:, k * V7X_LANES:(k + 1) * V7X_LANES]

    def mlp_first():
        h2buf[...] = _rmsnorm_mod(x1buf[...], gmlp_ref[...], modp_ref[3:4, :], modp_ref[4:5, :]).astype(BF16)

    first_tap = CONV_HALO - (CONV_WIDTH - 1)

    def conv_taps(gi):
        done = None
        for r0 in range(0, tm, CONV_ROWS):
            win = ubuf.at[gi, r0:r0 + CONV_ROWS + CONV_HALO, :]
            tap = jnp.zeros((CONV_ROWS, V7X_LANES), F32) + bdw_ref[gi]
            if done is not None:
                tap = _after(tap, done)
            for t in range(CONV_WIDTH):
                tap = tap + win[pl.ds(first_tap + t, CONV_ROWS, stride=1), :] * wdw_ref[gi, t:t + 1, :]
            cbuf[gi, r0:r0 + CONV_ROWS, :] = tap
            done = tap

    def loops(mixer, mlp):
        def hidden_phase(p, carry):
            if mlp:
                for c0 in range(0, fw, V7X_MXU_COLS):
                    t1 = jnp.dot(h2buf[...], m1_ref[p, :, c0:c0 + V7X_MXU_COLS], preferred_element_type=F32)
                    t1 = jnp.maximum(t1, 0.0)
                    tbuf[p, :, c0:c0 + V7X_MXU_COLS] = (t1 * t1).astype(BF16)
            if mixer:
                conv_taps(p)
            return carry

        lax.fori_loop(0, MLP_PHASES, hidden_phase, 0)

        def out_phase(p, carry):
            if mlp:
                ff = jnp.dot(tbuf[0], m2_ref[p, 0:fw, :], preferred_element_type=F32)
                for q in range(1, MLP_PHASES):
                    ff = ff + jnp.dot(tbuf[q], m2_ref[p, q * fw:(q + 1) * fw, :], preferred_element_type=F32)
            if mixer:
                conv_taps(MLP_PHASES + p)
            if mlp:
                fbuf[p] = ff
            return carry

        lax.fori_loop(0, MLP_PHASES, out_phase, 0)

    def mlp_last():
        ff = jnp.concatenate([fbuf[p] for p in range(MLP_PHASES)], axis=-1)
        o_ref[...] = x1buf[...] + modp_ref[5:6, :] * ff

    def mixer_last():
        cv = jnp.concatenate([cbuf[gi] for gi in range(groups)], axis=-1)
        mu = jnp.mean(cv, axis=-1, keepdims=True)
        cen = cv - mu
        var = jnp.mean(cen * cen, axis=-1, keepdims=True)
        v = cen * lax.rsqrt(var + EPS) * lng_ref[...] + lnb_ref[...]
        vb = _silu(v).astype(BF16)
        for c0 in range(0, d, V7X_MXU_COLS):
            cols = slice(c0, c0 + V7X_MXU_COLS)
            y = jnp.dot(vb, w2_ref[:, cols], preferred_element_type=F32) + b2_ref[:, cols]
            x1buf[:, cols] = x_ref[:, cols] + modc_ref[2:3, cols] * y

    @pl.when(n == 0)
    def _():
        for cp in weight_block_copies():
            cp.start()
        mixer_first()
        loops(True, False)
        mixer_last()
        for cp in weight_block_copies():
            cp.wait()

    @pl.when(jnp.logical_and(n > 0, n < n_chunks))
    def _():
        mixer_first()
        mlp_first()
        loops(True, True)
        mlp_last()
        mixer_last()

    @pl.when(n == n_chunks)
    def _():
        mlp_first()
        loops(False, True)
        mlp_last()

    if ns:
        @pl.when(n >= 1)
        def _():
            for k in range(ns):
                side_out(k, n - 1).wait()

        @pl.when(n < n_chunks)
        def _():
            for k in range(ns):
                side_in(k, n).wait()
                stage16[k][...] = stage32[k][...].astype(BF16)
                side_out(k, n).start()


def _conv_mlp_layer(x, mod, gmix, w1, b1, wdw, bdw, lng, lnb, w2, b2, gmlp, m1, m2, conv_layer, mlp_layer, side=()):
    b, s, d = x.shape
    d_ff = m1.shape[-1]
    fw = d_ff // MLP_PHASES
    ow = d // MLP_PHASES
    tm = CONV_TM
    tiles_per_seq = s // tm
    n_tiles = b * tiles_per_seq
    groups = d // V7X_LANES
    row = lambda v: v.reshape(1, -1)
    cur = lambda n: jnp.minimum(n, n_tiles - 1)
    prev = lambda n: jnp.maximum(n - 1, 0)
    side = [job for job in side if job[2] > 0]
    bf16_rows = 2 * V7X_SUBLANES
    for src, _, _ in side:
        assert src.shape[1] % (n_tiles * bf16_rows) == 0
    stage = lambda dtype: [pltpu.VMEM((count, src.shape[1] // n_tiles, src.shape[2]), dtype)
                           for src, _, count in side]
    any_spec = pl.BlockSpec(memory_space=pl.ANY)
    outs = pl.pallas_call(
        functools.partial(_conv_mlp_kernel, tiles_per_seq=tiles_per_seq, mlp_layer=mlp_layer,
                          side=tuple((first, count) for _, first, count in side)),
        out_shape=[jax.ShapeDtypeStruct((b * s, d), F32)]
                  + [jax.ShapeDtypeStruct((count,) + src.shape[1:], BF16) for src, _, count in side],
        grid=(n_tiles + 1,),
        in_specs=[
            pl.BlockSpec((tm, d), lambda n: (cur(n), 0)),
            pl.BlockSpec((None, 6, d), lambda n: (cur(n) // tiles_per_seq, 0, 0)),
            pl.BlockSpec((None, 6, d), lambda n: (prev(n) // tiles_per_seq, 0, 0)),
            _resident((1, d)),
            _resident((d, 2 * d), conv_layer),
            _resident((1, 2 * d)),
            _resident((groups, CONV_WIDTH, V7X_LANES)),
            _resident((groups, 1, V7X_LANES)),
            _resident((1, d)),
            _resident((1, d)),
            _resident((d, d), conv_layer),
            _resident((1, d)),
            _resident((1, d)),
            any_spec,
            any_spec,
        ] + [any_spec] * len(side),
        out_specs=[pl.BlockSpec((tm, d), lambda n: (prev(n), 0))] + [any_spec] * len(side),
        scratch_shapes=[
            pltpu.VMEM((groups, CONV_HALO + tm, V7X_LANES), F32),
            pltpu.VMEM((groups, tm, V7X_LANES), F32),
            pltpu.VMEM((tm, d), F32),
            pltpu.VMEM((tm, d), BF16),
            pltpu.VMEM((MLP_PHASES, tm, fw), BF16),
            pltpu.VMEM((MLP_PHASES, tm, ow), F32),
            pltpu.VMEM((MLP_PHASES, d, fw), BF16),
            pltpu.VMEM((MLP_PHASES, d_ff, ow), BF16),
            pltpu.SemaphoreType.DMA((2 * MLP_PHASES,)),
        ] + stage(F32) + stage(BF16) + ([pltpu.SemaphoreType.DMA((2 * len(side),))] if side else []),
        compiler_params=pltpu.CompilerParams(
            dimension_semantics=("arbitrary",),
            vmem_limit_bytes=V7X_VMEM_LIMIT_BYTES),
        name="conv_mlp",
    )(x.reshape(b * s, d), mod, mod, row(gmix), w1, row(b1),
      wdw.reshape(CONV_WIDTH, groups, V7X_LANES).transpose(1, 0, 2), bdw.reshape(groups, 1, V7X_LANES),
      row(lng), row(lnb), w2, row(b2), row(gmlp), m1, m2, *[src for src, _, _ in side])
    return outs[0].reshape(b, s, d), list(outs[1:])


def _mlp_kernel(x_ref, mod_ref, g_ref, w1_ref, w2_ref, fg_ref, o_ref, *, final):
    d_ff = w1_ref.shape[1]
    fw = d_ff // MLP_PHASES
    x = x_ref[...]
    h = _rmsnorm_mod(x, g_ref[...], mod_ref[3:4, :], mod_ref[4:5, :]).astype(BF16)
    acc = jnp.zeros(x.shape, F32)
    for c0 in range(0, d_ff, fw):
        t = jnp.maximum(jnp.dot(h, w1_ref[:, c0:c0 + fw], preferred_element_type=F32), 0.0)
        acc = acc + jnp.dot((t * t).astype(BF16), w2_ref[c0:c0 + fw, :], preferred_element_type=F32)
    y = x + mod_ref[5:6, :] * acc
    if final:
        y = y * lax.rsqrt(jnp.mean(y * y, axis=-1, keepdims=True) + EPS) * fg_ref[...]
    o_ref[...] = y


def _mlp_layer(x, mod, g, w1, w2, layer, fg, final):
    b, s, d = x.shape
    d_ff = w1.shape[-1]
    tm = MLP_TM
    return pl.pallas_call(
        functools.partial(_mlp_kernel, final=final),
        out_shape=jax.ShapeDtypeStruct(x.shape, F32),
        grid=(b, s // tm),
        in_specs=[
            pl.BlockSpec((None, tm, d), lambda bi, j: (bi, j, 0)),
            pl.BlockSpec((None, 6, d), lambda bi, j: (bi, 0, 0)),
            _resident((1, d)),
            _resident((d, d_ff), layer),
            _resident((d_ff, d), layer),
            _resident((1, d)),
        ],
        out_specs=pl.BlockSpec((None, tm, d), lambda bi, j: (bi, j, 0)),
        compiler_params=pltpu.CompilerParams(
            dimension_semantics=("arbitrary", "arbitrary"),
            vmem_limit_bytes=V7X_VMEM_LIMIT_BYTES),
        name="mlp_final" if final else "mlp",
    )(x, mod, g.reshape(1, d), w1, w2, fg.reshape(1, d))


def _log_gamma(head):
    return float(np.log(np.float32(1.0) - np.float32(2.0) ** np.float32(-5.0 - head)))


def _ret_kernel(x_ref, xnext_ref, mod_ref, modnext_ref, g_ref, cos_ref, sin_ref, win_ref, gng_ref, gnb_ref,
                wout_ref, o_ref, hbuf, hnext, proj, ybuf, state, dmask, *, tiles_per_seq):
    L, d = x_ref.shape
    heads = RET_HEADS
    dk = d // heads
    dv = 2 * d // heads
    half = dk // 2
    n = pl.program_id(0)

    @pl.when(n == 0)
    def _():
        hnext[...] = _rmsnorm_mod(x_ref[...], g_ref[...], mod_ref[0:1, :], mod_ref[1:2, :]).astype(BF16)
        r = lax.broadcasted_iota(jnp.int32, (L, L), 0)
        c = lax.broadcasted_iota(jnp.int32, (L, L), 1)
        dist = jnp.abs(r - c).astype(F32)
        chunk_shift = CHUNK.bit_length() - 1
        visible = jnp.right_shift(c, chunk_shift) <= jnp.right_shift(r, chunk_shift)
        for hd in range(heads):
            dmask[hd] = jnp.where(visible, jnp.exp(_log_gamma(hd) * dist), 0.0)

    @pl.when(n % tiles_per_seq == 0)
    def _():
        state[...] = jnp.zeros(state.shape, F32)

    hbuf[...] = hnext[...]
    x = x_ref[...]
    h = hbuf[...]
    n_in = win_ref.shape[1]
    for c0 in range(0, n_in, d):
        proj[:, c0:c0 + d] = jnp.dot(h, win_ref[:, c0:c0 + d], preferred_element_type=F32)

    hn = _rmsnorm_mod(xnext_ref[...], g_ref[...], modnext_ref[0:1, :], modnext_ref[1:2, :]).astype(BF16)
    hnext[...] = hn

    cos = cos_ref[...]
    sin = sin_ref[...]
    last = hn[L - 2 * V7X_SUBLANES:L, d - V7X_LANES:d].astype(F32)[0:V7X_SUBLANES, :]
    cos = jnp.concatenate([_after(cos[0:V7X_SUBLANES, :], last), cos[V7X_SUBLANES:, :]], axis=0)
    idx = lax.broadcasted_iota(jnp.int32, (L, 1), 0).astype(F32)
    k_off, v_off, g_off = d, 2 * d, 2 * d + heads * dv

    def rope(base):
        x1 = proj[:, base:base + half]
        x2 = proj[:, base + half:base + dk]
        return jnp.concatenate([x1 * cos - x2 * sin, x2 * cos + x1 * sin], axis=-1)

    for hd in range(heads):
        lg = _log_gamma(hd)
        q = rope(hd * dk)
        k = rope(k_off + hd * dk) * (dk ** -0.5)
        vb = proj[:, v_off + hd * dv:v_off + (hd + 1) * dv].astype(BF16)
        qb = q.astype(BF16)
        scores = lax.dot_general(qb, k.astype(BF16), (((1,), (1,)), ((), ())),
                                 preferred_element_type=F32) * dmask[hd]
        intra = jnp.dot(scores.astype(BF16), vb, preferred_element_type=F32)
        xi = jnp.exp(lg * (idx + 1.0))
        st = state[hd]
        cross = jnp.dot(qb, st.astype(BF16), preferred_element_type=F32) * xi
        zeta = jnp.exp(lg * (float(L - 1) - idx))
        kz = (k * zeta).astype(BF16)
        state[hd] = st * float(np.exp(np.float32(lg) * np.float32(L))) + lax.dot_general(
            kz, vb, (((0,), (0,)), ((), ())), preferred_element_type=F32)
        y = intra + cross
        mu = jnp.mean(y, axis=-1, keepdims=True)
        cen = y - mu
        var = jnp.mean(cen * cen, axis=-1, keepdims=True)
        yn = cen * lax.rsqrt(var + EPS) * gng_ref[hd:hd + 1, :] + gnb_ref[hd:hd + 1, :]
        gate = proj[:, g_off + hd * dv:g_off + (hd + 1) * dv]
        ybuf[:, hd * dv:(hd + 1) * dv] = (_silu(gate) * yn).astype(BF16)

    out = jnp.dot(ybuf[...], wout_ref[...], preferred_element_type=F32)
    o_ref[...] = x + mod_ref[2:3, :] * out


def _ret_layer(x, mod, g, cos, sin, w_in, gn_g, gn_b, w_out, ret_layer):
    b, s, d = x.shape
    L = RET_L
    heads = RET_HEADS
    dk, dv = d // heads, 2 * d // heads
    n_in = w_in.shape[-1]
    tiles_per_seq = s // L
    n_tiles = b * tiles_per_seq
    nxt = lambda n: jnp.minimum(n + 1, n_tiles - 1)
    x2 = x.reshape(b * s, d)
    out = pl.pallas_call(
        functools.partial(_ret_kernel, tiles_per_seq=tiles_per_seq),
        out_shape=jax.ShapeDtypeStruct((b * s, d), F32),
        grid=(n_tiles,),
        in_specs=[
            pl.BlockSpec((L, d), lambda n: (n, 0)),
            pl.BlockSpec((L, d), lambda n: (nxt(n), 0)),
            pl.BlockSpec((None, 6, d), lambda n: (n // tiles_per_seq, 0, 0)),
            pl.BlockSpec((None, 6, d), lambda n: (nxt(n) // tiles_per_seq, 0, 0)),
            _resident((1, d)),
            pl.BlockSpec((L, dk // 2), lambda n: (n % tiles_per_seq, 0)),
            pl.BlockSpec((L, dk // 2), lambda n: (n % tiles_per_seq, 0)),
            _resident((d, n_in), ret_layer),
            _resident((heads, dv)),
            _resident((heads, dv)),
            _resident((heads * dv, d), ret_layer),
        ],
        out_specs=pl.BlockSpec((L, d), lambda n: (n, 0)),
        scratch_shapes=[
            pltpu.VMEM((L, d), BF16),
            pltpu.VMEM((L, d), BF16),
            pltpu.VMEM((L, n_in), F32),
            pltpu.VMEM((L, heads * dv), BF16),
            pltpu.VMEM((heads, dk, dv), F32),
            pltpu.VMEM((heads, L, L), F32),
        ],
        compiler_params=pltpu.CompilerParams(
            dimension_semantics=("arbitrary",),
            vmem_limit_bytes=V7X_VMEM_LIMIT_BYTES),
        name="retention_mixer",
    )(x2, x2, mod, mod, g.reshape(1, d), cos, sin, w_in, gn_g, gn_b, w_out)
    return out.reshape(b, s, d)


def _rope_tables(seq, dk):
    pos = jnp.arange(seq, dtype=F32)
    inv = ROPE_BASE ** (-jnp.arange(0, dk, 2, dtype=F32) / dk)
    ang = pos[:, None] * inv[None, :]
    return jnp.cos(ang), jnp.sin(ang)


def kernel(x, c, ada_w, ada_b, norm_mix_g, norm_mlp_g, conv_w_pw1, conv_b_pw1, conv_w_dw, conv_b_dw, conv_ln_g, conv_ln_b, conv_w_pw2, conv_b_pw2, ret_w_in, ret_gn_g, ret_gn_b, ret_w_out, mlp_w1, mlp_w2, final_norm_g):
    depth = ada_w.shape[0]
    b, s, d = x.shape
    assert s % CONV_TM == 0 and s % MLP_TM == 0 and s % RET_L == 0 and RET_L % CHUNK == 0
    assert CONV_HALO >= CONV_WIDTH - 1 and CONV_HALO % V7X_SUBLANES == 0 and CONV_TM % CONV_ROWS == 0
    assert d == 2 * MLP_PHASES * V7X_LANES

    mod = _ada(c, ada_w, ada_b).reshape(depth, b, 6, d)
    cos, sin = _rope_tables(s, d // RET_HEADS)
    stacks = {"pw1": conv_w_pw1, "pw2": conv_w_pw2, "w_in": ret_w_in, "w_out": ret_w_out, "m1": mlp_w1, "m2": mlp_w2}
    first_later = {"pw1": 1, "pw2": 1, "w_in": 0, "w_out": 0, "m1": 1, "m2": 1}
    jobs = {name: (w, first_later[name], w.shape[0] - first_later[name]) for name, w in stacks.items()}
    names = [name for name, job in jobs.items() if job[2] > 0]
    assert depth % 2 == 0
    for i in range(depth):
        jm = i // 2
        if i == 0:
            own = [stacks[name][0:1].astype(BF16) for name in ("pw1", "pw2", "m1", "m2")]
            x, copies = _conv_mlp_layer(x, mod[i], norm_mix_g[i], own[0], conv_b_pw1[jm], conv_w_dw[jm], conv_b_dw[jm],
                                        conv_ln_g[jm], conv_ln_b[jm], own[1], conv_b_pw2[jm], norm_mlp_g[i],
                                        own[2], own[3], conv_layer=0, mlp_layer=0, side=[jobs[name] for name in names])
            later = dict(zip(names, copies))
            pw1, pw2, w_in, w_out, m1, m2 = (later.get(name) for name in stacks)
        elif i % 2 == 0:
            x, _ = _conv_mlp_layer(x, mod[i], norm_mix_g[i], pw1, conv_b_pw1[jm], conv_w_dw[jm], conv_b_dw[jm],
                                   conv_ln_g[jm], conv_ln_b[jm], pw2, conv_b_pw2[jm], norm_mlp_g[i],
                                   m1, m2, conv_layer=jm - 1, mlp_layer=i - 1)
        else:
            x = _ret_layer(x, mod[i], norm_mix_g[i], cos, sin, w_in, ret_gn_g[jm], ret_gn_b[jm], w_out, ret_layer=jm)
            x = _mlp_layer(x, mod[i], norm_mlp_g[i], m1, m2, i - 1, final_norm_g, final=(i == depth - 1))
    return x
```

```python
import functools

import numpy as np
import jax
import jax.numpy as jnp
from jax import lax
from jax.experimental import pallas as pl
from jax.experimental.pallas import tpu as pltpu

F32 = jnp.float32
BF16 = jnp.bfloat16

EPS = 1e-6
CHUNK = 64
CONV_WIDTH = 31
RET_HEADS = 4
ROPE_BASE = 10000.0

V7X_SUBLANES = 8
V7X_LANES = 128
V7X_MXU_COLS = 256
V7X_VMEM_LIMIT_BYTES = 56 * 1024 * 1024

CONV_TM = 512
CONV_HALO = 32
CONV_ROWS = 16
MLP_PHASES = 4
MLP_TM = 1024
RET_L = 256


def _resident(shape, layer=None):
    zeros = (0,) * len(shape)
    if layer is None:
        return pl.BlockSpec(shape, lambda *_: zeros, pipeline_mode=pl.Buffered(1))
    return pl.BlockSpec((None,) + tuple(shape), lambda *_: (layer,) + zeros, pipeline_mode=pl.Buffered(1))


def _rmsnorm_mod(x, g, shift, scale):
    y = x * lax.rsqrt(jnp.mean(x * x, axis=-1, keepdims=True) + EPS)
    return (y * g) * (1.0 + scale) + shift


def _silu(v):
    return v * jax.nn.sigmoid(v)


def _after(value, dep):
    bits = pltpu.bitcast(dep, jnp.uint32)
    zero = lax.shift_right_logical(lax.shift_right_logical(bits, jnp.uint32(16)), jnp.uint32(16))
    return pltpu.bitcast(pltpu.bitcast(value, jnp.uint32) + zero, F32)


def _ada_kernel(c_ref, w_ref, b_ref, o_ref):
    cond = _silu(c_ref[...])
    o_ref[...] = jnp.dot(cond.astype(BF16), w_ref[...].astype(BF16),
                         preferred_element_type=F32) + b_ref[...]


def _ada(c, ada_w, ada_b, layers):
    depth, d, n = ada_w.shape
    b = c.shape[0]
    tn = n // 4
    return pl.pallas_call(
        _ada_kernel,
        out_shape=jax.ShapeDtypeStruct((layers, b, n), F32),
        grid=(layers, n // tn),
        in_specs=[
            pl.BlockSpec((b, d), lambda l, j: (0, 0)),
            pl.BlockSpec((None, d, tn), lambda l, j: (l, 0, j)),
            pl.BlockSpec((None, 1, tn), lambda l, j: (l, 0, j)),
        ],
        out_specs=pl.BlockSpec((None, b, tn), lambda l, j: (l, 0, j)),
        compiler_params=pltpu.CompilerParams(
            dimension_semantics=("arbitrary", "arbitrary"),
            vmem_limit_bytes=V7X_VMEM_LIMIT_BYTES),
        name="ada_mod",
    )(c, ada_w, ada_b.reshape(depth, 1, n))


def _conv_mlp_kernel(*refs, tiles_per_seq, mlp_layer, side, ada):
    refs = list(refs)
    take = lambda count: [refs.pop(0) for _ in range(count)]
    ns = len(side)
    (x_ref, modc_ref, modp_ref, gmix_ref, w1_ref, b1_ref, wdw_ref, bdw_ref,
     lng_ref, lnb_ref, w2_ref, b2_ref, gmlp_ref, m1_hbm, m2_hbm) = take(15)
    side_src = take(ns)
    cblk_ref, adaw_hbm, adab_ref = take(3) if ada else (None, None, None)
    o_ref, = take(1)
    side_dst = take(ns)
    modl_ref = take(1)[0] if ada else None
    ubuf, cbuf, x1buf, h2buf, tbuf, fbuf, m1_ref, m2_ref, wsem = take(9)
    stage32 = take(ns)
    stage16 = take(ns)
    csem = take(1)[0] if ns else None
    adastage, adasem = take(2) if ada else (None, None)
    assert not refs
    tm, d = x_ref.shape
    groups = d // V7X_LANES
    fw = m1_ref.shape[2]
    ow = m2_ref.shape[2]
    n = pl.program_id(0)
    n_chunks = pl.num_programs(0) - 1

    def side_rows(k, j):
        first, count = side[k]
        rows = stage32[k].shape[1]
        return slice(first, first + count), pl.ds(pl.multiple_of(j * rows, rows), rows)

    def side_in(k, j):
        layers, rows = side_rows(k, j)
        return pltpu.make_async_copy(side_src[k].at[layers, rows, :], stage32[k], csem.at[k])

    def side_out(k, j):
        _, rows = side_rows(k, j)
        return pltpu.make_async_copy(stage16[k], side_dst[k].at[:, rows, :], csem.at[ns + k])

    if ns:
        @pl.when(n < n_chunks)
        def _():
            for k in range(ns):
                side_in(k, n).start()

    def ada_in(j):
        first, count = ada
        rows = adastage.shape[1]
        return pltpu.make_async_copy(
            adaw_hbm.at[first:first + count, pl.ds(pl.multiple_of(j * rows, rows), rows), :], adastage, adasem.at[0])

    if ada:
        @pl.when(n < n_chunks)
        def _():
            ada_in(n).start()

    def weight_block_copies():
        copies = []
        for q in range(MLP_PHASES):
            copies.append(pltpu.make_async_copy(
                m1_hbm.at[mlp_layer, :, pl.ds(q * fw, fw)], m1_ref.at[q], wsem.at[q]))
            copies.append(pltpu.make_async_copy(
                m2_hbm.at[mlp_layer, :, pl.ds(q * ow, ow)], m2_ref.at[q], wsem.at[MLP_PHASES + q]))
        return copies

    def mixer_first():
        @pl.when(n % tiles_per_seq == 0)
        def _():
            ubuf[:, 0:CONV_HALO, :] = jnp.zeros((groups, CONV_HALO, V7X_LANES), F32)

        @pl.when(n % tiles_per_seq != 0)
        def _():
            ubuf[:, 0:CONV_HALO, :] = ubuf[:, tm:tm + CONV_HALO, :]

        h = _rmsnorm_mod(x_ref[...], gmix_ref[...], modc_ref[0:1, :], modc_ref[1:2, :]).astype(BF16)
        for c0 in range(0, d, V7X_MXU_COLS):
            a = jnp.dot(h, w1_ref[:, c0:c0 + V7X_MXU_COLS], preferred_element_type=F32) + b1_ref[:, c0:c0 + V7X_MXU_COLS]
            gt = (jnp.dot(h, w1_ref[:, d + c0:d + c0 + V7X_MXU_COLS], preferred_element_type=F32)
                  + b1_ref[:, d + c0:d + c0 + V7X_MXU_COLS])
            u = a * jax.nn.sigmoid(gt)
            for k in range(V7X_MXU_COLS // V7X_LANES):
                ubuf[c0 // V7X_LANES + k, CONV_HALO:CONV_HALO + tm, :] = u[:, k * V7X_LANES:(k + 1) * V7X_LANES]

    def mlp_first():
        h2buf[...] = _rmsnorm_mod(x1buf[...], gmlp_ref[...], modp_ref[3:4, :], modp_ref[4:5, :]).astype(BF16)

    first_tap = CONV_HALO - (CONV_WIDTH - 1)

    def conv_taps(gi):
        done = None
        for r0 in range(0, tm, CONV_ROWS):
            win = ubuf.at[gi, r0:r0 + CONV_ROWS + CONV_HALO, :]
            tap = jnp.zeros((CONV_ROWS, V7X_LANES), F32) + bdw_ref[gi]
            if done is not None:
                tap = _after(tap, done)
            for t in range(CONV_WIDTH):
                tap = tap + win[pl.ds(first_tap + t, CONV_ROWS, stride=1), :] * wdw_ref[gi, t:t + 1, :]
            cbuf[gi, r0:r0 + CONV_ROWS, :] = tap
            done = tap

    def loops(mixer, mlp):
        def hidden_phase(p, carry):
            if mlp:
                for c0 in range(0, fw, V7X_MXU_COLS):
                    t1 = jnp.dot(h2buf[...], m1_ref[p, :, c0:c0 + V7X_MXU_COLS], preferred_element_type=F32)
                    t1 = jnp.maximum(t1, 0.0)
                    tbuf[p, :, c0:c0 + V7X_MXU_COLS] = (t1 * t1).astype(BF16)
            if mixer:
                conv_taps(p)
            return carry

        lax.fori_loop(0, MLP_PHASES, hidden_phase, 0)

        def out_phase(p, carry):
            if mlp:
                ff = jnp.dot(tbuf[0], m2_ref[p, 0:fw, :], preferred_element_type=F32)
                for q in range(1, MLP_PHASES):
                    ff = ff + jnp.dot(tbuf[q], m2_ref[p, q * fw:(q + 1) * fw, :], preferred_element_type=F32)
            if mixer:
                conv_taps(MLP_PHASES + p)
            if mlp:
                fbuf[p] = ff
            return carry

        lax.fori_loop(0, MLP_PHASES, out_phase, 0)

    def mlp_last():
        ff = jnp.concatenate([fbuf[p] for p in range(MLP_PHASES)], axis=-1)
        o_ref[...] = x1buf[...] + modp_ref[5:6, :] * ff

    def mixer_last():
        cv = jnp.concatenate([cbuf[gi] for gi in range(groups)], axis=-1)
        mu = jnp.mean(cv, axis=-1, keepdims=True)
        cen = cv - mu
        var = jnp.mean(cen * cen, axis=-1, keepdims=True)
        v = cen * lax.rsqrt(var + EPS) * lng_ref[...] + lnb_ref[...]
        vb = _silu(v).astype(BF16)
        for c0 in range(0, d, V7X_MXU_COLS):
            cols = slice(c0, c0 + V7X_MXU_COLS)
            y = jnp.dot(vb, w2_ref[:, cols], preferred_element_type=F32) + b2_ref[:, cols]
            x1buf[:, cols] = x_ref[:, cols] + modc_ref[2:3, cols] * y

    @pl.when(n == 0)
    def _():
        for cp in weight_block_copies():
            cp.start()
        mixer_first()
        loops(True, False)
        mixer_last()
        for cp in weight_block_copies():
            cp.wait()

    @pl.when(jnp.logical_and(n > 0, n < n_chunks))
    def _():
        mixer_first()
        mlp_first()
        loops(True, True)
        mlp_last()
        mixer_last()

    @pl.when(n == n_chunks)
    def _():
        mlp_first()
        loops(False, True)
        mlp_last()

    if ns:
        @pl.when(n >= 1)
        def _():
            for k in range(ns):
                side_out(k, n - 1).wait()

        @pl.when(n < n_chunks)
        def _():
            for k in range(ns):
                side_in(k, n).wait()
                stage16[k][...] = stage32[k][...].astype(BF16)
                side_out(k, n).start()

    if ada:
        @pl.when(n == 0)
        def _():
            modl_ref[...] = jnp.broadcast_to(adab_ref[...], modl_ref.shape)

        @pl.when(n < n_chunks)
        def _():
            ada_in(n).wait()
            cond = _silu(cblk_ref[...]).astype(BF16)
            for k in range(ada[1]):
                modl_ref[k] += jnp.dot(cond, adastage[k].astype(BF16), preferred_element_type=F32)


def _conv_mlp_layer(x, mod, gmix, w1, b1, wdw, bdw, lng, lnb, w2, b2, gmlp, m1, m2, conv_layer, mlp_layer,
                    side=(), ada=None):
    b, s, d = x.shape
    d_ff = m1.shape[-1]
    fw = d_ff // MLP_PHASES
    ow = d // MLP_PHASES
    tm = CONV_TM
    tiles_per_seq = s // tm
    n_tiles = b * tiles_per_seq
    groups = d // V7X_LANES
    row = lambda v: v.reshape(1, -1)
    cur = lambda n: jnp.minimum(n, n_tiles - 1)
    prev = lambda n: jnp.maximum(n - 1, 0)
    side = [job for job in side if job[2] > 0]
    bf16_rows = 2 * V7X_SUBLANES
    for src, _, _ in side:
        assert src.shape[1] % (n_tiles * bf16_rows) == 0
    stage = lambda dtype: [pltpu.VMEM((count, src.shape[1] // n_tiles, src.shape[2]), dtype)
                           for src, _, count in side]
    any_spec = pl.BlockSpec(memory_space=pl.ANY)
    ada_in, ada_specs, ada_out, ada_out_specs, ada_scratch, ada_job = [], [], [], [], [], None
    if ada is not None and ada[1].shape[0] > ada[3]:
        c, ada_w, ada_b, ada_first = ada
        ada_count, bc, n_mod = ada_w.shape[0] - ada_first, c.shape[0], ada_w.shape[2]
        rk = ada_w.shape[1] // n_tiles
        assert ada_w.shape[1] % n_tiles == 0 and rk % V7X_SUBLANES == 0
        ada_job = (ada_first, ada_count)
        ada_in = [c.reshape(bc, n_tiles, rk).transpose(1, 0, 2), ada_w, ada_b[ada_first:].reshape(ada_count, 1, n_mod)]
        ada_specs = [pl.BlockSpec((None, bc, rk), lambda n: (cur(n), 0, 0)), any_spec, _resident((ada_count, 1, n_mod))]
        ada_out = [jax.ShapeDtypeStruct((ada_count, bc, n_mod), F32)]
        ada_out_specs = [pl.BlockSpec((ada_count, bc, n_mod), lambda n: (0, 0, 0))]
        ada_scratch = [pltpu.VMEM((ada_count, rk, n_mod), F32), pltpu.SemaphoreType.DMA((1,))]
    outs = pl.pallas_call(
        functools.partial(_conv_mlp_kernel, tiles_per_seq=tiles_per_seq, mlp_layer=mlp_layer,
                          side=tuple((first, count) for _, first, count in side), ada=ada_job),
        out_shape=[jax.ShapeDtypeStruct((b * s, d), F32)]
                  + [jax.ShapeDtypeStruct((count,) + src.shape[1:], BF16) for src, _, count in side] + ada_out,
        grid=(n_tiles + 1,),
        in_specs=[
            pl.BlockSpec((tm, d), lambda n: (cur(n), 0)),
            pl.BlockSpec((None, 6, d), lambda n: (cur(n) // tiles_per_seq, 0, 0)),
            pl.BlockSpec((None, 6, d), lambda n: (prev(n) // tiles_per_seq, 0, 0)),
            _resident((1, d)),
            _resident((d, 2 * d), conv_layer),
            _resident((1, 2 * d)),
            _resident((groups, CONV_WIDTH, V7X_LANES)),
            _resident((groups, 1, V7X_LANES)),
            _resident((1, d)),
            _resident((1, d)),
            _resident((d, d), conv_layer),
            _resident((1, d)),
            _resident((1, d)),
            any_spec,
            any_spec,
        ] + [any_spec] * len(side) + ada_specs,
        out_specs=[pl.BlockSpec((tm, d), lambda n: (prev(n), 0))] + [any_spec] * len(side) + ada_out_specs,
        scratch_shapes=[
            pltpu.VMEM((groups, CONV_HALO + tm, V7X_LANES), F32),
            pltpu.VMEM((groups, tm, V7X_LANES), F32),
            pltpu.VMEM((tm, d), F32),
            pltpu.VMEM((tm, d), BF16),
            pltpu.VMEM((MLP_PHASES, tm, fw), BF16),
            pltpu.VMEM((MLP_PHASES, tm, ow), F32),
            pltpu.VMEM((MLP_PHASES, d, fw), BF16),
            pltpu.VMEM((MLP_PHASES, d_ff, ow), BF16),
            pltpu.SemaphoreType.DMA((2 * MLP_PHASES,)),
        ] + stage(F32) + stage(BF16) + ([pltpu.SemaphoreType.DMA((2 * len(side),))] if side else []) + ada_scratch,
        compiler_params=pltpu.CompilerParams(
            dimension_semantics=("arbitrary",),
            vmem_limit_bytes=V7X_VMEM_LIMIT_BYTES),
        name="conv_mlp",
    )(x.reshape(b * s, d), mod, mod, row(gmix), w1, row(b1),
      wdw.reshape(CONV_WIDTH, groups, V7X_LANES).transpose(1, 0, 2), bdw.reshape(groups, 1, V7X_LANES),
      row(lng), row(lnb), w2, row(b2), row(gmlp), m1, m2, *[src for src, _, _ in side], *ada_in)
    return outs[0].reshape(b, s, d), list(outs[1:1 + len(side)]), (outs[1 + len(side)] if ada_job else None)


def _mlp_kernel(x_ref, mod_ref, g_ref, w1_ref, w2_ref, fg_ref, o_ref, *, final):
    d_ff = w1_ref.shape[1]
    fw = d_ff // MLP_PHASES
    x = x_ref[...]
    h = _rmsnorm_mod(x, g_ref[...], mod_ref[3:4, :], mod_ref[4:5, :]).astype(BF16)
    acc = jnp.zeros(x.shape, F32)
    for c0 in range(0, d_ff, fw):
        t = jnp.maximum(jnp.dot(h, w1_ref[:, c0:c0 + fw], preferred_element_type=F32), 0.0)
        acc = acc + jnp.dot((t * t).astype(BF16), w2_ref[c0:c0 + fw, :], preferred_element_type=F32)
    y = x + mod_ref[5:6, :] * acc
    if final:
        y = y * lax.rsqrt(jnp.mean(y * y, axis=-1, keepdims=True) + EPS) * fg_ref[...]
    o_ref[...] = y


def _mlp_layer(x, mod, g, w1, w2, layer, fg, final):
    b, s, d = x.shape
    d_ff = w1.shape[-1]
    tm = MLP_TM
    return pl.pallas_call(
        functools.partial(_mlp_kernel, final=final),
        out_shape=jax.ShapeDtypeStruct(x.shape, F32),
        grid=(b, s // tm),
        in_specs=[
            pl.BlockSpec((None, tm, d), lambda bi, j: (bi, j, 0)),
            pl.BlockSpec((None, 6, d), lambda bi, j: (bi, 0, 0)),
            _resident((1, d)),
            _resident((d, d_ff), layer),
            _resident((d_ff, d), layer),
            _resident((1, d)),
        ],
        out_specs=pl.BlockSpec((None, tm, d), lambda bi, j: (bi, j, 0)),
        compiler_params=pltpu.CompilerParams(
            dimension_semantics=("arbitrary", "arbitrary"),
            vmem_limit_bytes=V7X_VMEM_LIMIT_BYTES),
        name="mlp_final" if final else "mlp",
    )(x, mod, g.reshape(1, d), w1, w2, fg.reshape(1, d))


def _log_gamma(head):
    return float(np.log(np.float32(1.0) - np.float32(2.0) ** np.float32(-5.0 - head)))


def _ret_kernel(x_ref, xnext_ref, mod_ref, modnext_ref, g_ref, cos_ref, sin_ref, win_ref, gng_ref, gnb_ref,
                wout_ref, o_ref, hbuf, hnext, proj, ybuf, state, dmask, *, tiles_per_seq):
    L, d = x_ref.shape
    heads = RET_HEADS
    dk = d // heads
    dv = 2 * d // heads
    half = dk // 2
    n = pl.program_id(0)

    @pl.when(n == 0)
    def _():
        hnext[...] = _rmsnorm_mod(x_ref[...], g_ref[...], mod_ref[0:1, :], mod_ref[1:2, :]).astype(BF16)
        r = lax.broadcasted_iota(jnp.int32, (L, L), 0)
        c = lax.broadcasted_iota(jnp.int32, (L, L), 1)
        dist = jnp.abs(r - c).astype(F32)
        chunk_shift = CHUNK.bit_length() - 1
        visible = jnp.right_shift(c, chunk_shift) <= jnp.right_shift(r, chunk_shift)
        for hd in range(heads):
            dmask[hd] = jnp.where(visible, jnp.exp(_log_gamma(hd) * dist), 0.0)

    @pl.when(n % tiles_per_seq == 0)
    def _():
        state[...] = jnp.zeros(state.shape, F32)

    hbuf[...] = hnext[...]
    x = x_ref[...]
    h = hbuf[...]
    n_in = win_ref.shape[1]
    for c0 in range(0, n_in, d):
        proj[:, c0:c0 + d] = jnp.dot(h, win_ref[:, c0:c0 + d], preferred_element_type=F32)

    hn = _rmsnorm_mod(xnext_ref[...], g_ref[...], modnext_ref[0:1, :], modnext_ref[1:2, :]).astype(BF16)
    hnext[...] = hn

    cos = cos_ref[...]
    sin = sin_ref[...]
    last = hn[L - 2 * V7X_SUBLANES:L, d - V7X_LANES:d].astype(F32)[0:V7X_SUBLANES, :]
    cos = jnp.concatenate([_after(cos[0:V7X_SUBLANES, :], last), cos[V7X_SUBLANES:, :]], axis=0)
    idx = lax.broadcasted_iota(jnp.int32, (L, 1), 0).astype(F32)
    k_off, v_off, g_off = d, 2 * d, 2 * d + heads * dv

    def rope(base):
        x1 = proj[:, base:base + half]
        x2 = proj[:, base + half:base + dk]
        return jnp.concatenate([x1 * cos - x2 * sin, x2 * cos + x1 * sin], axis=-1)

    for hd in range(heads):
        lg = _log_gamma(hd)
        q = rope(hd * dk)
        k = rope(k_off + hd * dk) * (dk ** -0.5)
        vb = proj[:, v_off + hd * dv:v_off + (hd + 1) * dv].astype(BF16)
        qb = q.astype(BF16)
        scores = lax.dot_general(qb, k.astype(BF16), (((1,), (1,)), ((), ())),
                                 preferred_element_type=F32) * dmask[hd]
        intra = jnp.dot(scores.astype(BF16), vb, preferred_element_type=F32)
        xi = jnp.exp(lg * (idx + 1.0))
        st = state[hd]
        cross = jnp.dot(qb, st.astype(BF16), preferred_element_type=F32) * xi
        zeta = jnp.exp(lg * (float(L - 1) - idx))
        kz = (k * zeta).astype(BF16)
        state[hd] = st * float(np.exp(np.float32(lg) * np.float32(L))) + lax.dot_general(
            kz, vb, (((0,), (0,)), ((), ())), preferred_element_type=F32)
        y = intra + cross
        mu = jnp.mean(y, axis=-1, keepdims=True)
        cen = y - mu
        var = jnp.mean(cen * cen, axis=-1, keepdims=True)
        yn = cen * lax.rsqrt(var + EPS) * gng_ref[hd:hd + 1, :] + gnb_ref[hd:hd + 1, :]
        gate = proj[:, g_off + hd * dv:g_off + (hd + 1) * dv]
        ybuf[:, hd * dv:(hd + 1) * dv] = (_silu(gate) * yn).astype(BF16)

    out = jnp.dot(ybuf[...], wout_ref[...], preferred_element_type=F32)
    o_ref[...] = x + mod_ref[2:3, :] * out


def _ret_layer(x, mod, g, cos, sin, w_in, gn_g, gn_b, w_out, ret_layer):
    b, s, d = x.shape
    L = RET_L
    heads = RET_HEADS
    dk, dv = d // heads, 2 * d // heads
    n_in = w_in.shape[-1]
    tiles_per_seq = s // L
    n_tiles = b * tiles_per_seq
    nxt = lambda n: jnp.minimum(n + 1, n_tiles - 1)
    x2 = x.reshape(b * s, d)
    out = pl.pallas_call(
        functools.partial(_ret_kernel, tiles_per_seq=tiles_per_seq),
        out_shape=jax.ShapeDtypeStruct((b * s, d), F32),
        grid=(n_tiles,),
        in_specs=[
            pl.BlockSpec((L, d), lambda n: (n, 0)),
            pl.BlockSpec((L, d), lambda n: (nxt(n), 0)),
            pl.BlockSpec((None, 6, d), lambda n: (n // tiles_per_seq, 0, 0)),
            pl.BlockSpec((None, 6, d), lambda n: (nxt(n) // tiles_per_seq, 0, 0)),
            _resident((1, d)),
            pl.BlockSpec((L, dk // 2), lambda n: (n % tiles_per_seq, 0)),
            pl.BlockSpec((L, dk // 2), lambda n: (n % tiles_per_seq, 0)),
            _resident((d, n_in), ret_layer),
            _resident((heads, dv)),
            _resident((heads, dv)),
            _resident((heads * dv, d), ret_layer),
        ],
        out_specs=pl.BlockSpec((L, d), lambda n: (n, 0)),
        scratch_shapes=[
            pltpu.VMEM((L, d), BF16),
            pltpu.VMEM((L, d), BF16),
            pltpu.VMEM((L, n_in), F32),
            pltpu.VMEM((L, heads * dv), BF16),
            pltpu.VMEM((heads, dk, dv), F32),
            pltpu.VMEM((heads, L, L), F32),
        ],
        compiler_params=pltpu.CompilerParams(
            dimension_semantics=("arbitrary",),
            vmem_limit_bytes=V7X_VMEM_LIMIT_BYTES),
        name="retention_mixer",
    )(x2, x2, mod, mod, g.reshape(1, d), cos, sin, w_in, gn_g, gn_b, w_out)
    return out.reshape(b, s, d)


def _rope_tables(seq, dk):
    pos = jnp.arange(seq, dtype=F32)
    inv = ROPE_BASE ** (-jnp.arange(0, dk, 2, dtype=F32) / dk)
    ang = pos[:, None] * inv[None, :]
    return jnp.cos(ang), jnp.sin(ang)


def kernel(x, c, ada_w, ada_b, norm_mix_g, norm_mlp_g, conv_w_pw1, conv_b_pw1, conv_w_dw, conv_b_dw, conv_ln_g, conv_ln_b, conv_w_pw2, conv_b_pw2, ret_w_in, ret_gn_g, ret_gn_b, ret_w_out, mlp_w1, mlp_w2, final_norm_g):
    depth = ada_w.shape[0]
    b, s, d = x.shape
    assert s % CONV_TM == 0 and s % MLP_TM == 0 and s % RET_L == 0 and RET_L % CHUNK == 0
    assert CONV_HALO >= CONV_WIDTH - 1 and CONV_HALO % V7X_SUBLANES == 0 and CONV_TM % CONV_ROWS == 0
    assert d == 2 * MLP_PHASES * V7X_LANES

    mod0 = _ada(c, ada_w, ada_b, 1).reshape(1, b, 6, d)
    cos, sin = _rope_tables(s, d // RET_HEADS)
    stacks = {"pw1": conv_w_pw1, "pw2": conv_w_pw2, "w_in": ret_w_in, "w_out": ret_w_out, "m1": mlp_w1, "m2": mlp_w2}
    first_later = {"pw1": 1, "pw2": 1, "w_in": 0, "w_out": 0, "m1": 1, "m2": 1}
    jobs = {name: (w, first_later[name], w.shape[0] - first_later[name]) for name, w in stacks.items()}
    names = [name for name, job in jobs.items() if job[2] > 0]
    assert depth % 2 == 0
    for i in range(depth):
        jm = i // 2
        if i == 0:
            own = [stacks[name][0:1].astype(BF16) for name in ("pw1", "pw2", "m1", "m2")]
            x, copies, mod_later = _conv_mlp_layer(
                x, mod0[0], norm_mix_g[i], own[0], conv_b_pw1[jm], conv_w_dw[jm], conv_b_dw[jm],
                conv_ln_g[jm], conv_ln_b[jm], own[1], conv_b_pw2[jm], norm_mlp_g[i],
                own[2], own[3], conv_layer=0, mlp_layer=0, side=[jobs[name] for name in names],
                ada=(c, ada_w, ada_b, 1))
            later = dict(zip(names, copies))
            pw1, pw2, w_in, w_out, m1, m2 = (later.get(name) for name in stacks)
            mod_later = mod_later.reshape(depth - 1, b, 6, d)
            mod = [mod0[0]] + [mod_later[k] for k in range(depth - 1)]
        elif i % 2 == 0:
            x, _, _ = _conv_mlp_layer(x, mod[i], norm_mix_g[i], pw1, conv_b_pw1[jm], conv_w_dw[jm], conv_b_dw[jm],
                                      conv_ln_g[jm], conv_ln_b[jm], pw2, conv_b_pw2[jm], norm_mlp_g[i],
                                      m1, m2, conv_layer=jm - 1, mlp_layer=i - 1)
        else:
            x = _ret_layer(x, mod[i], norm_mix_g[i], cos, sin, w_in, ret_gn_g[jm], ret_gn_b[jm], w_out, ret_layer=jm)
            x = _mlp_layer(x, mod[i], norm_mlp_g[i], m1, m2, i - 1, final_norm_g, final=(i == depth - 1))
    return x
```

```python
import functools

import numpy as np
import jax
import jax.numpy as jnp
from jax import lax
from jax.experimental import pallas as pl
from jax.experimental.pallas import tpu as pltpu

F32 = jnp.float32
BF16 = jnp.bfloat16

EPS = 1e-6
CHUNK = 64
CONV_WIDTH = 31
RET_HEADS = 4
ROPE_BASE = 10000.0

V7X_SUBLANES = 8
V7X_LANES = 128
V7X_MXU_COLS = 256
V7X_VMEM_LIMIT_BYTES = 56 * 1024 * 1024

CONV_TM = 512
CONV_HALO = 32
CONV_ROWS = 16
MLP_PHASES = 4
MLP_TM = 1024
RET_L = 256


def _resident(shape, layer=None):
    zeros = (0,) * len(shape)
    if layer is None:
        return pl.BlockSpec(shape, lambda *_: zeros, pipeline_mode=pl.Buffered(1))
    return pl.BlockSpec((None,) + tuple(shape), lambda *_: (layer,) + zeros, pipeline_mode=pl.Buffered(1))


def _rmsnorm_mod(x, g, shift, scale):
    y = x * lax.rsqrt(jnp.mean(x * x, axis=-1, keepdims=True) + EPS)
    return (y * g) * (1.0 + scale) + shift


def _silu(v):
    return v * jax.nn.sigmoid(v)


def _after(value, dep):
    bits = pltpu.bitcast(dep, jnp.uint32)
    zero = lax.shift_right_logical(lax.shift_right_logical(bits, jnp.uint32(16)), jnp.uint32(16))
    return pltpu.bitcast(pltpu.bitcast(value, jnp.uint32) + zero, F32)


def _ada_kernel(c_ref, w_ref, b_ref, o_ref):
    cond = _silu(c_ref[...])
    o_ref[...] = jnp.dot(cond.astype(BF16), w_ref[...].astype(BF16),
                         preferred_element_type=F32) + b_ref[...]


def _ada(c, ada_w, ada_b):
    depth, d, n = ada_w.shape
    b = c.shape[0]
    tn = n // 4
    return pl.pallas_call(
        _ada_kernel,
        out_shape=jax.ShapeDtypeStruct((depth, b, n), F32),
        grid=(depth, n // tn),
        in_specs=[
            pl.BlockSpec((b, d), lambda l, j: (0, 0)),
            pl.BlockSpec((None, d, tn), lambda l, j: (l, 0, j)),
            pl.BlockSpec((None, 1, tn), lambda l, j: (l, 0, j)),
        ],
        out_specs=pl.BlockSpec((None, b, tn), lambda l, j: (l, 0, j)),
        compiler_params=pltpu.CompilerParams(
            dimension_semantics=("arbitrary", "arbitrary"),
            vmem_limit_bytes=V7X_VMEM_LIMIT_BYTES),
        name="ada_mod",
    )(c, ada_w, ada_b.reshape(depth, 1, n))


def _conv_mlp_kernel(*refs, tiles_per_seq, mlp_layer, side):
    (x_ref, modc_ref, modp_ref, gmix_ref, w1_ref, b1_ref, wdw_ref, bdw_ref,
     lng_ref, lnb_ref, w2_ref, b2_ref, gmlp_ref, m1_hbm, m2_hbm) = refs[:15]
    ns = len(side)
    side_src = refs[15:15 + ns]
    o_ref = refs[15 + ns]
    side_dst = refs[16 + ns:16 + 2 * ns]
    ubuf, cbuf, x1buf, h2buf, tbuf, fbuf, m1_ref, m2_ref, wsem = refs[16 + 2 * ns:25 + 2 * ns]
    stage32 = refs[25 + 2 * ns:25 + 3 * ns]
    stage16 = refs[25 + 3 * ns:25 + 4 * ns]
    csem = refs[25 + 4 * ns] if ns else None
    tm, d = x_ref.shape
    groups = d // V7X_LANES
    fw = m1_ref.shape[2]
    ow = m2_ref.shape[2]
    n = pl.program_id(0)
    n_chunks = pl.num_programs(0) - 1

    def side_rows(k, j):
        first, count = side[k]
        rows = stage32[k].shape[1]
        return slice(first, first + count), pl.ds(pl.multiple_of(j * rows, rows), rows)

    def side_in(k, j):
        layers, rows = side_rows(k, j)
        return pltpu.make_async_copy(side_src[k].at[layers, rows, :], stage32[k], csem.at[k])

    def side_out(k, j):
        _, rows = side_rows(k, j)
        return pltpu.make_async_copy(stage16[k], side_dst[k].at[:, rows, :], csem.at[ns + k])

    if ns:
        @pl.when(n < n_chunks)
        def _():
            for k in range(ns):
                side_in(k, n).start()

    def weight_block_copies():
        copies = []
        for q in range(MLP_PHASES):
            copies.append(pltpu.make_async_copy(
                m1_hbm.at[mlp_layer, :, pl.ds(q * fw, fw)], m1_ref.at[q], wsem.at[q]))
            copies.append(pltpu.make_async_copy(
                m2_hbm.at[mlp_layer, :, pl.ds(q * ow, ow)], m2_ref.at[q], wsem.at[MLP_PHASES + q]))
        return copies

    def mixer_first():
        @pl.when(n % tiles_per_seq == 0)
        def _():
            ubuf[:, 0:CONV_HALO, :] = jnp.zeros((groups, CONV_HALO, V7X_LANES), F32)

        @pl.when(n % tiles_per_seq != 0)
        def _():
            ubuf[:, 0:CONV_HALO, :] = ubuf[:, tm:tm + CONV_HALO, :]

        h = _rmsnorm_mod(x_ref[...], gmix_ref[...], modc_ref[0:1, :], modc_ref[1:2, :]).astype(BF16)
        for c0 in range(0, d, V7X_MXU_COLS):
            a = jnp.dot(h, w1_ref[:, c0:c0 + V7X_MXU_COLS], preferred_element_type=F32) + b1_ref[:, c0:c0 + V7X_MXU_COLS]
            gt = (jnp.dot(h, w1_ref[:, d + c0:d + c0 + V7X_MXU_COLS], preferred_element_type=F32)
                  + b1_ref[:, d + c0:d + c0 + V7X_MXU_COLS])
            u = a * jax.nn.sigmoid(gt)
            for k in range(V7X_MXU_COLS // V7X_LANES):
                ubuf[c0 // V7X_LANES + k, CONV_HALO:CONV_HALO + tm, :] = u[:, k * V7X_LANES:(k + 1) * V7X_LANES]

    def mlp_first():
        h2buf[...] = _rmsnorm_mod(x1buf[...], gmlp_ref[...], modp_ref[3:4, :], modp_ref[4:5, :]).astype(BF16)

    first_tap = CONV_HALO - (CONV_WIDTH - 1)

    def conv_taps(gi):
        done = None
        for r0 in range(0, tm, CONV_ROWS):
            win = ubuf.at[gi, r0:r0 + CONV_ROWS + CONV_HALO, :]
            tap = jnp.zeros((CONV_ROWS, V7X_LANES), F32) + bdw_ref[gi]
            if done is not None:
                tap = _after(tap, done)
            for t in range(CONV_WIDTH):
                tap = tap + win[pl.ds(first_tap + t, CONV_ROWS, stride=1), :] * wdw_ref[gi, t:t + 1, :]
            cbuf[gi, r0:r0 + CONV_ROWS, :] = tap
            done = tap

    def loops(mixer, mlp):
        def hidden_phase(p, carry):
            if mlp:
                for c0 in range(0, fw, V7X_MXU_COLS):
                    t1 = jnp.dot(h2buf[...], m1_ref[p, :, c0:c0 + V7X_MXU_COLS], preferred_element_type=F32)
                    t1 = jnp.maximum(t1, 0.0)
                    tbuf[p, :, c0:c0 + V7X_MXU_COLS] = (t1 * t1).astype(BF16)
            if mixer:
                conv_taps(p)
            return carry

        lax.fori_loop(0, MLP_PHASES, hidden_phase, 0)

        def out_phase(p, carry):
            if mlp:
                ff = jnp.dot(tbuf[0], m2_ref[p, 0:fw, :], preferred_element_type=F32)
                for q in range(1, MLP_PHASES):
                    ff = ff + jnp.dot(tbuf[q], m2_ref[p, q * fw:(q + 1) * fw, :], preferred_element_type=F32)
            if mixer:
                conv_taps(MLP_PHASES + p)
            if mlp:
                fbuf[p] = ff
            return carry

        lax.fori_loop(0, MLP_PHASES, out_phase, 0)

    def mlp_last():
        ff = jnp.concatenate([fbuf[p] for p in range(MLP_PHASES)], axis=-1)
        o_ref[...] = x1buf[...] + modp_ref[5:6, :] * ff

    def mixer_last():
        cv = jnp.concatenate([cbuf[gi] for gi in range(groups)], axis=-1)
        mu = jnp.mean(cv, axis=-1, keepdims=True)
        cen = cv - mu
        var = jnp.mean(cen * cen, axis=-1, keepdims=True)
        v = cen * lax.rsqrt(var + EPS) * lng_ref[...] + lnb_ref[...]
        vb = _silu(v).astype(BF16)
        for c0 in range(0, d, V7X_MXU_COLS):
            cols = slice(c0, c0 + V7X_MXU_COLS)
            y = jnp.dot(vb, w2_ref[:, cols], preferred_element_type=F32) + b2_ref[:, cols]
            x1buf[:, cols] = x_ref[:, cols] + modc_ref[2:3, cols] * y

    @pl.when(n == 0)
    def _():
        for cp in weight_block_copies():
            cp.start()
        mixer_first()
        loops(True, False)
        mixer_last()
        for cp in weight_block_copies():
            cp.wait()

    @pl.when(jnp.logical_and(n > 0, n < n_chunks))
    def _():
        mixer_first()
        mlp_first()
        loops(True, True)
        mlp_last()
        mixer_last()

    @pl.when(n == n_chunks)
    def _():
        mlp_first()
        loops(False, True)
        mlp_last()

    if ns:
        @pl.when(n >= 1)
        def _():
            for k in range(ns):
                side_out(k, n - 1).wait()

        @pl.when(n < n_chunks)
        def _():
            for k in range(ns):
                side_in(k, n).wait()
                stage16[k][...] = stage32[k][...].astype(BF16)
                side_out(k, n).start()


def _conv_mlp_layer(x, mod, gmix, w1, b1, wdw, bdw, lng, lnb, w2, b2, gmlp, m1, m2, conv_layer, mlp_layer, side=()):
    b, s, d = x.shape
    d_ff = m1.shape[-1]
    fw = d_ff // MLP_PHASES
    ow = d // MLP_PHASES
    tm = CONV_TM
    tiles_per_seq = s // tm
    n_tiles = b * tiles_per_seq
    groups = d // V7X_LANES
    row = lambda v: v.reshape(1, -1)
    cur = lambda n: jnp.minimum(n, n_tiles - 1)
    prev = lambda n: jnp.maximum(n - 1, 0)
    side = [job for job in side if job[2] > 0]
    bf16_rows = 2 * V7X_SUBLANES
    for src, _, _ in side:
        assert src.shape[1] % (n_tiles * bf16_rows) == 0
    stage = lambda dtype: [pltpu.VMEM((count, src.shape[1] // n_tiles, src.shape[2]), dtype)
                           for src, _, count in side]
    any_spec = pl.BlockSpec(memory_space=pl.ANY)
    outs = pl.pallas_call(
        functools.partial(_conv_mlp_kernel, tiles_per_seq=tiles_per_seq, mlp_layer=mlp_layer,
                          side=tuple((first, count) for _, first, count in side)),
        out_shape=[jax.ShapeDtypeStruct((b * s, d), F32)]
                  + [jax.ShapeDtypeStruct((count,) + src.shape[1:], BF16) for src, _, count in side],
        grid=(n_tiles + 1,),
        in_specs=[
            pl.BlockSpec((tm, d), lambda n: (cur(n), 0)),
            pl.BlockSpec((None, 6, d), lambda n: (cur(n) // tiles_per_seq, 0, 0)),
            pl.BlockSpec((None, 6, d), lambda n: (prev(n) // tiles_per_seq, 0, 0)),
            _resident((1, d)),
            _resident((d, 2 * d), conv_layer),
            _resident((1, 2 * d)),
            _resident((groups, CONV_WIDTH, V7X_LANES)),
            _resident((groups, 1, V7X_LANES)),
            _resident((1, d)),
            _resident((1, d)),
            _resident((d, d), conv_layer),
            _resident((1, d)),
            _resident((1, d)),
            any_spec,
            any_spec,
        ] + [any_spec] * len(side),
        out_specs=[pl.BlockSpec((tm, d), lambda n: (prev(n), 0))] + [any_spec] * len(side),
        scratch_shapes=[
            pltpu.VMEM((groups, CONV_HALO + tm, V7X_LANES), F32),
            pltpu.VMEM((groups, tm, V7X_LANES), F32),
            pltpu.VMEM((tm, d), F32),
            pltpu.VMEM((tm, d), BF16),
            pltpu.VMEM((MLP_PHASES, tm, fw), BF16),
            pltpu.VMEM((MLP_PHASES, tm, ow), F32),
            pltpu.VMEM((MLP_PHASES, d, fw), BF16),
            pltpu.VMEM((MLP_PHASES, d_ff, ow), BF16),
            pltpu.SemaphoreType.DMA((2 * MLP_PHASES,)),
        ] + stage(F32) + stage(BF16) + ([pltpu.SemaphoreType.DMA((2 * len(side),))] if side else []),
        compiler_params=pltpu.CompilerParams(
            dimension_semantics=("arbitrary",),
            vmem_limit_bytes=V7X_VMEM_LIMIT_BYTES),
        name="conv_mlp",
    )(x.reshape(b * s, d), mod, mod, row(gmix), w1, row(b1),
      wdw.reshape(CONV_WIDTH, groups, V7X_LANES).transpose(1, 0, 2), bdw.reshape(groups, 1, V7X_LANES),
      row(lng), row(lnb), w2, row(b2), row(gmlp), m1, m2, *[src for src, _, _ in side])
    return outs[0].reshape(b, s, d), list(outs[1:])


def _mlp_kernel(x_ref, mod_ref, g_ref, w1_ref, w2_ref, fg_ref, o_ref, *, final):
    d_ff = w1_ref.shape[1]
    fw = d_ff // MLP_PHASES
    x = x_ref[...]
    h = _rmsnorm_mod(x, g_ref[...], mod_ref[3:4, :], mod_ref[4:5, :]).astype(BF16)
    acc = jnp.zeros(x.shape, F32)
    for c0 in range(0, d_ff, fw):
        t = jnp.maximum(jnp.dot(h, w1_ref[:, c0:c0 + fw], preferred_element_type=F32), 0.0)
        acc = acc + jnp.dot((t * t).astype(BF16), w2_ref[c0:c0 + fw, :], preferred_element_type=F32)
    y = x + mod_ref[5:6, :] * acc
    if final:
        y = y * lax.rsqrt(jnp.mean(y * y, axis=-1, keepdims=True) + EPS) * fg_ref[...]
    o_ref[...] = y


def _mlp_layer(x, mod, g, w1, w2, layer, fg, final):
    b, s, d = x.shape
    d_ff = w1.shape[-1]
    tm = MLP_TM
    return pl.pallas_call(
        functools.partial(_mlp_kernel, final=final),
        out_shape=jax.ShapeDtypeStruct(x.shape, F32),
        grid=(b, s // tm),
        in_specs=[
            pl.BlockSpec((None, tm, d), lambda bi, j: (bi, j, 0)),
            pl.BlockSpec((None, 6, d), lambda bi, j: (bi, 0, 0)),
            _resident((1, d)),
            _resident((d, d_ff), layer),
            _resident((d_ff, d), layer),
            _resident((1, d)),
        ],
        out_specs=pl.BlockSpec((None, tm, d), lambda bi, j: (bi, j, 0)),
        compiler_params=pltpu.CompilerParams(
            dimension_semantics=("arbitrary", "arbitrary"),
            vmem_limit_bytes=V7X_VMEM_LIMIT_BYTES),
        name="mlp_final" if final else "mlp",
    )(x, mod, g.reshape(1, d), w1, w2, fg.reshape(1, d))


def _log_gamma(head):
    return float(np.log(np.float32(1.0) - np.float32(2.0) ** np.float32(-5.0 - head)))


def _ret_kernel(x_ref, xnext_ref, mod_ref, modnext_ref, g_ref, cos_ref, sin_ref, win_ref, gng_ref, gnb_ref,
                wout_ref, o_ref, hbuf, hnext, proj, ybuf, state, dmask, *, tiles_per_seq):
    L, d = x_ref.shape
    heads = RET_HEADS
    dk = d // heads
    dv = 2 * d // heads
    half = dk // 2
    n = pl.program_id(0)

    @pl.when(n == 0)
    def _():
        hnext[...] = _rmsnorm_mod(x_ref[...], g_ref[...], mod_ref[0:1, :], mod_ref[1:2, :]).astype(BF16)
        r = lax.broadcasted_iota(jnp.int32, (L, L), 0)
        c = lax.broadcasted_iota(jnp.int32, (L, L), 1)
        dist = jnp.abs(r - c).astype(F32)
        chunk_shift = CHUNK.bit_length() - 1
        visible = jnp.right_shift(c, chunk_shift) <= jnp.right_shift(r, chunk_shift)
        for hd in range(heads):
            dmask[hd] = jnp.where(visible, jnp.exp(_log_gamma(hd) * dist), 0.0)

    @pl.when(n % tiles_per_seq == 0)
    def _():
        state[...] = jnp.zeros(state.shape, F32)

    hbuf[...] = hnext[...]
    x = x_ref[...]
    h = hbuf[...]
    n_in = win_ref.shape[1]
    for c0 in range(0, n_in, d):
        proj[:, c0:c0 + d] = jnp.dot(h, win_ref[:, c0:c0 + d], preferred_element_type=F32)

    hn = _rmsnorm_mod(xnext_ref[...], g_ref[...], modnext_ref[0:1, :], modnext_ref[1:2, :]).astype(BF16)
    hnext[...] = hn

    cos = cos_ref[...]
    sin = sin_ref[...]
    last = hn[L - 2 * V7X_SUBLANES:L, d - V7X_LANES:d].astype(F32)[0:V7X_SUBLANES, :]
    cos = jnp.concatenate([_after(cos[0:V7X_SUBLANES, :], last), cos[V7X_SUBLANES:, :]], axis=0)
    idx = lax.broadcasted_iota(jnp.int32, (L, 1), 0).astype(F32)
    k_off, v_off, g_off = d, 2 * d, 2 * d + heads * dv

    def rope(base):
        x1 = proj[:, base:base + half]
        x2 = proj[:, base + half:base + dk]
        return jnp.concatenate([x1 * cos - x2 * sin, x2 * cos + x1 * sin], axis=-1)

    for hd in range(heads):
        lg = _log_gamma(hd)
        q = rope(hd * dk)
        k = rope(k_off + hd * dk) * (dk ** -0.5)
        vb = proj[:, v_off + hd * dv:v_off + (hd + 1) * dv].astype(BF16)
        qb = q.astype(BF16)
        scores = lax.dot_general(qb, k.astype(BF16), (((1,), (1,)), ((), ())),
                                 preferred_element_type=F32) * dmask[hd]
        intra = jnp.dot(scores.astype(BF16), vb, preferred_element_type=F32)
        xi = jnp.exp(lg * (idx + 1.0))
        st = state[hd]
        cross = jnp.dot(qb, st.astype(BF16), preferred_element_type=F32) * xi
        zeta = jnp.exp(lg * (float(L - 1) - idx))
        kz = (k * zeta).astype(BF16)
        state[hd] = st * float(np.exp(np.float32(lg) * np.float32(L))) + lax.dot_general(
            kz, vb, (((0,), (0,)), ((), ())), preferred_element_type=F32)
        y = intra + cross
        mu = jnp.mean(y, axis=-1, keepdims=True)
        cen = y - mu
        var = jnp.mean(cen * cen, axis=-1, keepdims=True)
        yn = cen * lax.rsqrt(var + EPS) * gng_ref[hd:hd + 1, :] + gnb_ref[hd:hd + 1, :]
        gate = proj[:, g_off + hd * dv:g_off + (hd + 1) * dv]
        ybuf[:, hd * dv:(hd + 1) * dv] = (_silu(gate) * yn).astype(BF16)

    out = jnp.dot(ybuf[...], wout_ref[...], preferred_element_type=F32)
    o_ref[...] = x + mod_ref[2:3, :] * out


def _ret_layer(x, mod, g, cos, sin, w_in, gn_g, gn_b, w_out, ret_layer):
    b, s, d = x.shape
    L = RET_L
    heads = RET_HEADS
    dk, dv = d // heads, 2 * d // heads
    n_in = w_in.shape[-1]
    tiles_per_seq = s // L
    n_tiles = b * tiles_per_seq
    nxt = lambda n: jnp.minimum(n + 1, n_tiles - 1)
    x2 = x.reshape(b * s, d)
    out = pl.pallas_call(
        functools.partial(_ret_kernel, tiles_per_seq=tiles_per_seq),
        out_shape=jax.ShapeDtypeStruct((b * s, d), F32),
        grid=(n_tiles,),
        in_specs=[
            pl.BlockSpec((L, d), lambda n: (n, 0)),
            pl.BlockSpec((L, d), lambda n: (nxt(n), 0)),
            pl.BlockSpec((None, 6, d), lambda n: (n // tiles_per_seq, 0, 0)),
            pl.BlockSpec((None, 6, d), lambda n: (nxt(n) // tiles_per_seq, 0, 0)),
            _resident((1, d)),
            pl.BlockSpec((L, dk // 2), lambda n: (n % tiles_per_seq, 0)),
            pl.BlockSpec((L, dk // 2), lambda n: (n % tiles_per_seq, 0)),
            _resident((d, n_in), ret_layer),
            _resident((heads, dv)),
            _resident((heads, dv)),
            _resident((heads * dv, d), ret_layer),
        ],
        out_specs=pl.BlockSpec((L, d), lambda n: (n, 0)),
        scratch_shapes=[
            pltpu.VMEM((L, d), BF16),
            pltpu.VMEM((L, d), BF16),
            pltpu.VMEM((L, n_in), F32),
            pltpu.VMEM((L, heads * dv), BF16),
            pltpu.VMEM((heads, dk, dv), F32),
            pltpu.VMEM((heads, L, L), F32),
        ],
        compiler_params=pltpu.CompilerParams(
            dimension_semantics=("arbitrary",),
            vmem_limit_bytes=V7X_VMEM_LIMIT_BYTES),
        name="retention_mixer",
    )(x2, x2, mod, mod, g.reshape(1, d), cos, sin, w_in, gn_g, gn_b, w_out)
    return out.reshape(b, s, d)


def _rope_tables(seq, dk):
    pos = np.arange(seq, dtype=np.float32)
    inv = (np.float32(ROPE_BASE) ** (-np.arange(0, dk, 2, dtype=np.float32) / np.float32(dk))).astype(np.float32)
    ang = pos[:, None] * inv[None, :]
    return jnp.asarray(np.cos(ang), F32), jnp.asarray(np.sin(ang), F32)


def kernel(x, c, ada_w, ada_b, norm_mix_g, norm_mlp_g, conv_w_pw1, conv_b_pw1, conv_w_dw, conv_b_dw, conv_ln_g, conv_ln_b, conv_w_pw2, conv_b_pw2, ret_w_in, ret_gn_g, ret_gn_b, ret_w_out, mlp_w1, mlp_w2, final_norm_g):
    depth = ada_w.shape[0]
    b, s, d = x.shape
    assert s % CONV_TM == 0 and s % MLP_TM == 0 and s % RET_L == 0 and RET_L % CHUNK == 0
    assert CONV_HALO >= CONV_WIDTH - 1 and CONV_HALO % V7X_SUBLANES == 0 and CONV_TM % CONV_ROWS == 0
    assert d == 2 * MLP_PHASES * V7X_LANES

    mod = _ada(c, ada_w, ada_b).reshape(depth, b, 6, d)
    cos, sin = _rope_tables(s, d // RET_HEADS)
    stacks = {"pw1": conv_w_pw1, "pw2": conv_w_pw2, "w_in": ret_w_in, "w_out": ret_w_out, "m1": mlp_w1, "m2": mlp_w2}
    first_later = {"pw1": 1, "pw2": 1, "w_in": 0, "w_out": 0, "m1": 1, "m2": 1}
    jobs = {name: (w, first_later[name], w.shape[0] - first_later[name]) for name, w in stacks.items()}
    names = [name for name, job in jobs.items() if job[2] > 0]
    assert depth % 2 == 0
    for i in range(depth):
        jm = i // 2
        if i == 0:
            own = [stacks[name][0:1].astype(BF16) for name in ("pw1", "pw2", "m1", "m2")]
            x, copies = _conv_mlp_layer(x, mod[i], norm_mix_g[i], own[0], conv_b_pw1[jm], conv_w_dw[jm], conv_b_dw[jm],
                                        conv_ln_g[jm], conv_ln_b[jm], own[1], conv_b_pw2[jm], norm_mlp_g[i],
                                        own[2], own[3], conv_layer=0, mlp_layer=0, side=[jobs[name] for name in names])
            later = dict(zip(names, copies))
            pw1, pw2, w_in, w_out, m1, m2 = (later.get(name) for name in stacks)
        elif i % 2 == 0:
            x, _ = _conv_mlp_layer(x, mod[i], norm_mix_g[i], pw1, conv_b_pw1[jm], conv_w_dw[jm], conv_b_dw[jm],
                                   conv_ln_g[jm], conv_ln_b[jm], pw2, conv_b_pw2[jm], norm_mlp_g[i],
                                   m1, m2, conv_layer=jm - 1, mlp_layer=i - 1)
        else:
            x = _ret_layer(x, mod[i], norm_mix_g[i], cos, sin, w_in, ret_gn_g[jm], ret_gn_b[jm], w_out, ret_layer=jm)
            x = _mlp_layer(x, mod[i], norm_mlp_g[i], m1, m2, i - 1, final_norm_g, final=(i == depth - 1))
    return x
```
